```python
import math, functools
import jax, jax.numpy as jnp
from jax import lax
import numpy as np

D_MODEL = 1024
BATCH = 4
SEQ = 4096
DEPTH = 1
DEC_BATCH = 32
DEC_SEQ = 8
PAST_LEN = 16384
PAGE_SIZE = 128

N_HEADS = 8
HEAD_DIM = D_MODEL // (2 * N_HEADS)
V_DIM = 2 * HEAD_DIM
Q_W = N_HEADS * 2 * HEAD_DIM
K_W = N_HEADS * 2 * HEAD_DIM
V_W = N_HEADS * V_DIM
ATTN_SCALE = HEAD_DIM ** -0.5
Q_BLOCK = 128
NEG_INF = -1e30
N_BUCKETS = 32
MAX_EXACT = N_BUCKETS // 2
MAX_DISTANCE = 128
LRU_W = D_MODEL
N_BLOCKS = N_HEADS * 2
BLOCK_W = LRU_W // N_BLOCKS
CONV_W = 4
LRU_C = 8.0
D_FF = ((8 * D_MODEL // 3 + 255) // 256) * 256
PLE_DIM = 256
EPS = 1e-6

kernel_name = "hawk_diffattn_parallel_decoder_step"


def rmsnorm(x, g):
    xf = x.astype(jnp.float32)
    y = xf * lax.rsqrt(jnp.mean(xf * xf, axis=-1, keepdims=True) + EPS)
    return (y * g.astype(jnp.float32)).astype(x.dtype)


def t5_bucket(q_pos, k_pos):
    n = jnp.maximum(q_pos[:, None] - k_pos[None, :], 0)
    nf = jnp.maximum(n, 1).astype(jnp.float32)
    large = MAX_EXACT + (jnp.log(nf / MAX_EXACT) / math.log(MAX_DISTANCE / MAX_EXACT)
                         * (N_BUCKETS - MAX_EXACT)).astype(jnp.int32)
    large = jnp.minimum(large, N_BUCKETS - 1)
    return jnp.where(n < MAX_EXACT, n, large)


def diff_attend(q, k, v, q_pos, k_pos, rel_bias, lam):
    bias = jnp.transpose(rel_bias[t5_bucket(q_pos, k_pos)], (2, 0, 1)).astype(jnp.float32)
    s = jnp.einsum('bqhcd,bkhcd->bchqk', q, k, preferred_element_type=jnp.float32) * ATTN_SCALE + bias
    mask = k_pos[None, :] <= q_pos[:, None]
    s = jnp.where(mask, s, NEG_INF)
    pr = jax.nn.softmax(s, axis=-1)
    w = pr[:, 0] - lam * pr[:, 1]
    return jnp.einsum('bhqk,bkhv->bqhv', w.astype(v.dtype), v)


def attend_prompt(q, k, v, lam, rel_bias):
    B, S = q.shape[0], q.shape[1]
    nb = S // Q_BLOCK
    qb = jnp.transpose(q.reshape(B, nb, Q_BLOCK, N_HEADS, 2, HEAD_DIM), (1, 0, 2, 3, 4, 5))
    k_pos = jnp.arange(S, dtype=jnp.int32)

    def one(args):
        q_blk, i = args
        q_pos = i * Q_BLOCK + jnp.arange(Q_BLOCK, dtype=jnp.int32)
        return diff_attend(q_blk, k, v, q_pos, k_pos, rel_bias, lam)

    o = lax.map(one, (qb, jnp.arange(nb, dtype=jnp.int32)))
    return jnp.transpose(o, (1, 0, 2, 3, 4)).reshape(B, S, N_HEADS, V_DIM)


def attend_sample(q, k, v, lam, cache_k, cache_v, layer, page_table, rel_bias):
    T = q.shape[1]
    past = page_table.shape[1] * cache_k.shape[2]
    q_pos = past + jnp.arange(T, dtype=jnp.int32)
    k_pos = jnp.arange(past + T, dtype=jnp.int32)

    def one(args):
        q_b, k_b, v_b, pages = args
        kp = cache_k[layer, pages].reshape(past, N_HEADS, 2, HEAD_DIM).astype(k_b.dtype)
        vp = cache_v[layer, pages].reshape(past, N_HEADS, V_DIM).astype(v_b.dtype)
        k_all = jnp.concatenate([kp, k_b], axis=0)[None]
        v_all = jnp.concatenate([vp, v_b], axis=0)[None]
        return diff_attend(q_b[None], k_all, v_all, q_pos, k_pos, rel_bias, lam)[0]

    return lax.map(one, (q, k, v, page_table))


def causal_conv(x, conv_state, w, b):
    S = x.shape[1]
    xp = jnp.concatenate([conv_state.astype(x.dtype), x], axis=1)
    y = b + sum(xp[:, j:j + S] * w[j] for j in range(CONV_W))
    return y, xp[:, -(CONV_W - 1):]


def rg_lru(x, pos, h0, w_gx, b_gx, w_ga, b_ga, a_param):
    B, S, W = x.shape
    xb = x.reshape(B, S, N_BLOCKS, BLOCK_W)
    gx = jax.nn.sigmoid((jnp.einsum('bsni,nij->bsnj', xb, w_gx).reshape(B, S, W) + b_gx).astype(jnp.float32))
    ga = jax.nn.sigmoid((jnp.einsum('bsni,nij->bsnj', xb, w_ga).reshape(B, S, W) + b_ga).astype(jnp.float32))
    log_a = -LRU_C * ga * jax.nn.softplus(-a_param.astype(jnp.float32))
    reset = (pos == 0)[None, :, None]
    a = jnp.where(reset, 0.0, jnp.exp(log_a))
    mult = jnp.where(reset, 1.0, jnp.sqrt(-jnp.expm1(2.0 * log_a)))
    b = x.astype(jnp.float32) * gx * mult
    b = b.at[:, 0].add(a[:, 0] * h0.astype(jnp.float32))

    def combine(c1, c2):
        a1, b1 = c1
        a2, b2 = c2
        return a1 * a2, a2 * b1 + b2

    _, h = lax.associative_scan(combine, (a, b), axis=1)
    return h, h[:, -1]


def trunk_layer(h, p_l, pos, conv0, h0, attend, lam_init,
                g_mix, w_in, conv_w, conv_b, w_gx, b_gx, w_ga, b_ga, a_param,
                lam_q1, lam_k1, lam_q2, lam_k2, g_subln, b_merge, w_out,
                g_ffn, w_ffn_in, w_ffn_out, g_ple, w_ple_gate, w_ple_proj):
    B, S, _ = h.shape
    u = rmsnorm(h, g_mix)
    proj = u @ w_in
    cuts = [int(c) for c in np.cumsum([Q_W, K_W, V_W, LRU_W, LRU_W, D_MODEL])]
    q, k, v, xr, yr, g_att, g_rnn = jnp.split(proj, cuts, axis=-1)
    q = q.reshape(B, S, N_HEADS, 2, HEAD_DIM)
    k = k.reshape(B, S, N_HEADS, 2, HEAD_DIM)
    v = v.reshape(B, S, N_HEADS, V_DIM)
    f32 = jnp.float32
    lam = (jnp.exp(jnp.sum(lam_q1.astype(f32) * lam_k1.astype(f32)))
           - jnp.exp(jnp.sum(lam_q2.astype(f32) * lam_k2.astype(f32))) + lam_init)
    o = attend(q, k, v, lam)
    o = (rmsnorm(o, g_subln) * (1.0 - lam_init)).reshape(B, S, V_W).astype(h.dtype)
    xc, conv_new = causal_conv(xr, conv0, conv_w, conv_b)
    hr, h_last = rg_lru(xc, pos, h0, w_gx, b_gx, w_ga, b_ga, a_param)
    r = hr.astype(h.dtype) * jax.nn.gelu(yr)
    mix = jax.nn.sigmoid(g_att + b_merge[0]) * o + jax.nn.sigmoid(g_rnn + b_merge[1]) * r
    h = h + mix @ w_out
    gate, up = jnp.split(rmsnorm(h, g_ffn) @ w_ffn_in, 2, axis=-1)
    h = h + (jax.nn.silu(gate) * up) @ w_ffn_out
    h = h + jax.nn.sigmoid(rmsnorm(h, g_ple) @ w_ple_gate) * (p_l @ w_ple_proj)
    return (h, k.reshape(B, S, N_HEADS, 2 * HEAD_DIM), v, conv_new, h_last)


def setup_inputs(seed: int = 0) -> dict:
    key = jax.random.key(seed)
    ks = iter(jax.random.split(key, 40))
    nrm = lambda shape, scale: jax.random.normal(next(ks), shape, jnp.float32) * scale
    gain = lambda shape: 1.0 + nrm(shape, 0.01)
    n_pages = PAST_LEN // PAGE_SIZE
    n_used = DEC_BATCH * n_pages
    n_pool = (n_used * 5) // 4
    page_table = jax.random.permutation(next(ks), n_pool)[:n_used].reshape(DEC_BATCH, n_pages).astype(jnp.int32)
    a8 = jax.random.uniform(next(ks), (DEPTH, LRU_W), jnp.float32, 0.9, 0.999)
    a0 = a8 ** (1.0 / LRU_C)
    a_param = jnp.log(a0) - jnp.log1p(-a0)
    in_w = Q_W + K_W + V_W + 2 * LRU_W + 2 * D_MODEL
    return {
        "x_prompt": nrm((BATCH, SEQ, D_MODEL), 1.0),
        "x_sample": nrm((DEC_BATCH, DEC_SEQ, D_MODEL), 1.0),
        "p_prompt": nrm((DEPTH, BATCH, SEQ, PLE_DIM), 1.0),
        "p_sample": nrm((DEPTH, DEC_BATCH, DEC_SEQ, PLE_DIM), 1.0),
        "cache_k": nrm((DEPTH, n_pool, PAGE_SIZE, N_HEADS, 2 * HEAD_DIM), 1.0),
        "cache_v": nrm((DEPTH, n_pool, PAGE_SIZE, N_HEADS, V_DIM), 1.0),
        "state_conv": nrm((DEPTH, DEC_BATCH, CONV_W - 1, LRU_W), 1.0),
        "state_rnn": nrm((DEPTH, DEC_BATCH, LRU_W), 0.5),
        "page_table": page_table,
        "rel_bias": nrm((N_BUCKETS, N_HEADS), 0.5),
        "g_mix": gain((DEPTH, D_MODEL)),
        "w_in": nrm((DEPTH, D_MODEL, in_w), D_MODEL ** -0.5),
        "conv_w": nrm((DEPTH, CONV_W, LRU_W), CONV_W ** -0.5),
        "conv_b": nrm((DEPTH, LRU_W), 0.01),
        "w_gx": nrm((DEPTH, N_BLOCKS, BLOCK_W, BLOCK_W), BLOCK_W ** -0.5),
        "b_gx": nrm((DEPTH, LRU_W), 0.01),
        "w_ga": nrm((DEPTH, N_BLOCKS, BLOCK_W, BLOCK_W), BLOCK_W ** -0.5),
        "b_ga": nrm((DEPTH, LRU_W), 0.01),
        "a_param": a_param,
        "lam_q1": nrm((DEPTH, HEAD_DIM), 0.1),
        "lam_k1": nrm((DEPTH, HEAD_DIM), 0.1),
        "lam_q2": nrm((DEPTH, HEAD_DIM), 0.1),
        "lam_k2": nrm((DEPTH, HEAD_DIM), 0.1),
        "g_subln": gain((DEPTH, V_DIM)),
        "b_merge": nrm((DEPTH, 2, D_MODEL), 0.01),
        "w_out": nrm((DEPTH, D_MODEL, D_MODEL), D_MODEL ** -0.5),
        "g_ffn": gain((DEPTH, D_MODEL)),
        "w_ffn_in": nrm((DEPTH, D_MODEL, 2 * D_FF), D_MODEL ** -0.5),
        "w_ffn_out": nrm((DEPTH, D_FF, D_MODEL), D_FF ** -0.5),
        "g_ple": gain((DEPTH, D_MODEL)),
        "w_ple_gate": nrm((DEPTH, D_MODEL, D_MODEL), D_MODEL ** -0.5),
        "w_ple_proj": nrm((DEPTH, PLE_DIM, D_MODEL), PLE_DIM ** -0.5),
        "g_final": gain((D_MODEL,)),
    }


def reference(x_prompt, x_sample, p_prompt, p_sample, cache_k, cache_v, state_conv, state_rnn, page_table,
              rel_bias, g_mix, w_in, conv_w, conv_b, w_gx, b_gx, w_ga, b_ga, a_param,
              lam_q1, lam_k1, lam_q2, lam_k2, g_subln, b_merge, w_out,
              g_ffn, w_ffn_in, w_ffn_out, g_ple, w_ple_gate, w_ple_proj, g_final):
    B, S, _ = x_prompt.shape
    T = x_sample.shape[1]
    past = page_table.shape[1] * cache_k.shape[2]
    pos_p = jnp.arange(S, dtype=jnp.int32)
    pos_s = past + jnp.arange(T, dtype=jnp.int32)
    hp, hs = x_prompt, x_sample
    kp_l, vp_l, cp_l, rp_l, ks_l, vs_l, cs_l, rs_l = [], [], [], [], [], [], [], []
    for l in range(DEPTH):
        lam_init = 0.8 - 0.6 * math.exp(-0.3 * l)
        lw = (g_mix[l], w_in[l], conv_w[l], conv_b[l], w_gx[l], b_gx[l], w_ga[l], b_ga[l], a_param[l],
              lam_q1[l], lam_k1[l], lam_q2[l], lam_k2[l], g_subln[l], b_merge[l], w_out[l],
              g_ffn[l], w_ffn_in[l], w_ffn_out[l], g_ple[l], w_ple_gate[l], w_ple_proj[l])
        att_p = functools.partial(attend_prompt, rel_bias=rel_bias)
        att_s = functools.partial(attend_sample, cache_k=cache_k, cache_v=cache_v, layer=l,
                                  page_table=page_table, rel_bias=rel_bias)
        conv0 = jnp.zeros((B, CONV_W - 1, LRU_W), x_prompt.dtype)
        h0 = jnp.zeros((B, LRU_W), jnp.float32)
        hp, kp, vp, cp, rp = trunk_layer(hp, p_prompt[l], pos_p, conv0, h0, att_p, lam_init, *lw)
        hs, ks_, vs_, cs, rs = trunk_layer(hs, p_sample[l], pos_s, state_conv[l], state_rnn[l], att_s, lam_init, *lw)
        kp_l.append(kp); vp_l.append(vp); cp_l.append(cp); rp_l.append(rp)
        ks_l.append(ks_); vs_l.append(vs_); cs_l.append(cs); rs_l.append(rs)
    y_prompt = rmsnorm(hp, g_final)
    y_sample = rmsnorm(hs, g_final)
    return (y_prompt, y_sample,
            jnp.stack(kp_l), jnp.stack(vp_l), jnp.stack(cp_l), jnp.stack(rp_l),
            jnp.stack(ks_l), jnp.stack(vs_l), jnp.stack(cs_l), jnp.stack(rs_l))
```

```python
import functools
import math

import jax
import jax.numpy as jnp
import numpy as np
from jax import lax
from jax.experimental import pallas as pl
from jax.experimental.pallas import tpu as pltpu

F32 = jnp.float32
BF16 = jnp.bfloat16

EPS = 1e-6
NEG_INF = -1e30
MAX_DISTANCE = 128
LRU_C = 8.0
LANES = 128
SUBLANES = 8
VMEM_LIMIT = 56 * 1024 * 1024


def _cparams(*sem):
    return pltpu.CompilerParams(dimension_semantics=sem, vmem_limit_bytes=VMEM_LIMIT)


def _rms(x, g):
    return x * lax.rsqrt(jnp.mean(x * x, axis=-1, keepdims=True) + EPS) * g


def _const_spec(shape):
    nd = len(shape)
    return pl.BlockSpec(shape, lambda *_: (0,) * nd, pipeline_mode=pl.Buffered(1))


def _inproj_kernel(x_ref, g_ref, w_ref, o_ref, xn_ref):
    @pl.when(pl.program_id(1) == 0)
    def _():
        xn_ref[...] = _rms(x_ref[...], g_ref[...]).astype(BF16)

    o_ref[...] = jnp.dot(xn_ref[...], w_ref[...], preferred_element_type=F32)


def _inproj(x, g, w_bf16, tm, tn):
    n, d = x.shape
    nw = w_bf16.shape[1]
    return pl.pallas_call(
        _inproj_kernel,
        grid=(n // tm, nw // tn),
        in_specs=[pl.BlockSpec((tm, d), lambda i, j: (i, 0)),
                  pl.BlockSpec((1, d), lambda i, j: (0, 0)),
                  pl.BlockSpec((d, tn), lambda i, j: (0, j))],
        out_specs=pl.BlockSpec((tm, tn), lambda i, j: (i, j)),
        out_shape=jax.ShapeDtypeStruct((n, nw), F32),
        scratch_shapes=[pltpu.VMEM((tm, d), BF16)],
        compiler_params=_cparams("parallel", "arbitrary"),
        name="inproj",
    )(x, g, w_bf16)


def _rglru_kernel(xr_ref, yr_ref, tail_ref, h0_ref, cw_ref, cb_ref, wg_ref, bgx_ref, bga_ref, ap_ref,
                  r_ref, hlast_ref, tail_scr, carry_scr, *, nseq, rows, reset_first):
    s = pl.program_id(1)
    w = xr_ref.shape[-1]

    @pl.when(s == 0)
    def _():
        tail_scr[...] = tail_ref[...]
        carry_scr[...] = h0_ref[...]

    x = xr_ref[...].reshape(nseq, rows, w)
    tail = tail_scr[...]
    cw = cw_ref[...]
    kw = cw.shape[0]
    row8 = lax.broadcasted_iota(jnp.int32, (1, SUBLANES, w), 1)
    xc = cb_ref[...] + x * cw[kw - 1:kw]
    for k in range(1, kw):
        xs = pltpu.roll(x, k, axis=1)
        head = jnp.where(row8 < k, pltpu.roll(tail, k, axis=1), xs[:, :SUBLANES, :])
        xs = jnp.concatenate([head, xs[:, SUBLANES:, :]], axis=1) if rows > SUBLANES else head
        xc = xc + xs * cw[kw - 1 - k:kw - k]
    tail_scr[...] = x[:, rows - SUBLANES:, :]

    xc2 = xc.reshape(nseq * rows, w)
    xcb = xc2.astype(BF16)
    gx_pre, ga_pre = [], []
    for g in range(w // LANES):
        pre = jnp.dot(xcb[:, g * LANES:(g + 1) * LANES], wg_ref[g], preferred_element_type=F32)
        gx_pre.append(pre[:, :LANES])
        ga_pre.append(pre[:, LANES:])
    gx = jax.nn.sigmoid(jnp.concatenate(gx_pre, axis=1) + bgx_ref[...])
    ga = jax.nn.sigmoid(jnp.concatenate(ga_pre, axis=1) + bga_ref[...])

    z = -ap_ref[...]
    softplus = jnp.maximum(z, 0.0) + jnp.log1p(jnp.exp(-jnp.abs(z)))
    log_a = -LRU_C * ga * softplus
    a = jnp.exp(log_a)
    mult = jnp.sqrt(-jnp.tanh(log_a) * (a * a + 1.0))
    if reset_first:
        t_in_seq = lax.broadcasted_iota(jnp.int32, (nseq * rows, w), 0)
        if nseq > 1:
            t_in_seq = t_in_seq % rows
        first = t_in_seq == jnp.where(s == 0, 0, -1)
        a = jnp.where(first, 0.0, a)
        mult = jnp.where(first, 1.0, mult)
    b = xc2 * gx * mult

    ngrp = nseq * rows // SUBLANES
    av = a.reshape(ngrp, SUBLANES, w)
    bv = b.reshape(ngrp, SUBLANES, w)
    d = 1
    while d < SUBLANES:
        keep = row8 >= d
        a_prev = jnp.where(keep, pltpu.roll(av, d, axis=1), 1.0)
        b_prev = jnp.where(keep, pltpu.roll(bv, d, axis=1), 0.0)
        bv = av * b_prev + bv
        av = av * a_prev
        d *= 2
    av = av.reshape(nseq, rows, w)
    bv = bv.reshape(nseq, rows, w)
    carry = carry_scr[...]
    hs = []
    for g in range(rows // SUBLANES):
        sl = slice(g * SUBLANES, (g + 1) * SUBLANES)
        hg = av[:, sl, :] * carry + bv[:, sl, :]
        carry = hg[:, SUBLANES - 1:, :]
        hs.append(hg)
    carry_scr[...] = carry
    hlast_ref[...] = hs[-1]
    h = jnp.concatenate(hs, axis=1) if len(hs) > 1 else hs[0]
    r_ref[...] = h.reshape(nseq * rows, w) * jax.nn.gelu(yr_ref[...])


def _rglru(proj, col_x, col_y, tail0, h0, cw, cb, wg, bgx, bga, ap, *, n_seq, seq_len, rows, reset_first):
    w = cw.shape[1]
    n = n_seq * seq_len
    if seq_len >= rows:
        assert seq_len % rows == 0
        nseq_t, steps, grid0 = 1, seq_len // rows, n_seq
    else:
        assert seq_len == rows == SUBLANES
        nseq_t, steps, grid0 = n_seq, 1, 1
    tm = nseq_t * rows
    kern = functools.partial(_rglru_kernel, nseq=nseq_t, rows=rows, reset_first=reset_first)
    small = lambda shape: pl.BlockSpec(shape, lambda b, s: (0,) * len(shape))
    return pl.pallas_call(
        kern,
        grid=(grid0, steps),
        in_specs=[pl.BlockSpec((tm, w), lambda b, s: (b * steps + s, col_x)),
                  pl.BlockSpec((tm, w), lambda b, s: (b * steps + s, col_y)),
                  pl.BlockSpec((nseq_t, SUBLANES, w), lambda b, s: (b, 0, 0)),
                  pl.BlockSpec((nseq_t, 1, w), lambda b, s: (b, 0, 0)),
                  small(cw.shape), small(cb.shape), small(wg.shape),
                  small(bgx.shape), small(bga.shape), small(ap.shape)],
        out_specs=[pl.BlockSpec((tm, w), lambda b, s: (b * steps + s, 0)),
                   pl.BlockSpec((nseq_t, SUBLANES, w), lambda b, s: (b, 0, 0))],
        out_shape=[jax.ShapeDtypeStruct((n, w), F32),
                   jax.ShapeDtypeStruct((n_seq, SUBLANES, w), F32)],
        scratch_shapes=[pltpu.VMEM((nseq_t, SUBLANES, w), F32),
                        pltpu.VMEM((nseq_t, 1, w), F32)],
        compiler_params=_cparams("parallel", "arbitrary"),
        name="rglru",
    )(proj, proj, tail0, h0, cw, cb, wg, bgx, bga, ap)


def _lam(lq1, lk1, lq2, lk2, lam_init):
    return (jnp.exp(jnp.sum(lq1[...] * lk1[...], axis=-1, keepdims=True))
            - jnp.exp(jnp.sum(lq2[...] * lk2[...], axis=-1, keepdims=True)) + lam_init)


def _two_map_queries(q, hd):
    lane = lax.broadcasted_iota(jnp.int32, q.shape, 1)
    return jnp.concatenate([jnp.where(lane < hd, q, 0.0), jnp.where(lane >= hd, q, 0.0)], axis=0).astype(BF16)


def _online_softmax_step(s, m, l, acc, v):
    m_new = jnp.maximum(m, jnp.max(s, axis=1, keepdims=True))
    alpha = jnp.exp(m - m_new)
    p = jnp.exp(s - m_new)
    l = alpha * l + jnp.sum(p, axis=1, keepdims=True)
    acc = alpha * acc + jnp.dot(p.astype(BF16), v, preferred_element_type=F32)
    return m_new, l, acc


def _diff_finish(l, acc, t, lam, g, lam_init):
    o = acc / l
    o = o[:t] - lam * o[t:]
    return _rms(o, g) * (1.0 - lam_init)


_NT = (((1,), (1,)), ((), ()))


def _attn_prompt_kernel(q_ref, k_ref, v_ref, bd_ref, bp_ref, lq1, lk1, lq2, lk2, gs_ref, o_ref,
                        kb_ref, vb_ref, *, t, hd, scale, lam_init):
    qi = pl.program_id(2)

    @pl.when(qi == 0)
    def _():
        kb_ref[...] = k_ref[...].astype(BF16)
        vb_ref[...] = v_ref[...].astype(BF16)

    qq = _two_map_queries(q_ref[...] * scale, hd)

    def chunk(j, carry, bias):
        start = pl.multiple_of(j * t, t)
        s = lax.dot_general(qq, kb_ref[pl.ds(start, t), :], _NT, preferred_element_type=F32)
        if bias is not None:
            s = s + jnp.concatenate([bias, bias], axis=0)
        return _online_softmax_step(s, *carry, vb_ref[pl.ds(start, t), :])

    carry = (jnp.full((2 * t, 1), NEG_INF, F32), jnp.zeros((2 * t, 1), F32), jnp.zeros((2 * t, 2 * hd), F32))
    carry = lax.fori_loop(0, jnp.maximum(qi - 1, 0), lambda j, c: chunk(j, c, None), carry)
    carry = lax.cond(qi >= 1, lambda c: chunk(qi - 1, c, bp_ref[0]), lambda c: c, carry)
    _, l, acc = chunk(qi, carry, bd_ref[0])
    o_ref[...] = _diff_finish(l, acc, t, _lam(lq1, lk1, lq2, lk2, lam_init), gs_ref[...], lam_init)


def _attn_prompt(proj, bias_diag, bias_prev, lams, gs, *, n_seq, seq_len, n_heads, hd, t, lam_init):
    n, vw = n_seq * seq_len, n_heads * 2 * hd
    nq = seq_len // t
    hw = 2 * hd
    kern = functools.partial(_attn_prompt_kernel, t=t, hd=hd, scale=hd ** -0.5, lam_init=lam_init)
    vec = lambda a: pl.BlockSpec(a.shape, lambda h, b, i: (0, 0))
    return pl.pallas_call(
        kern,
        grid=(n_heads, n_seq, nq),
        in_specs=[pl.BlockSpec((t, hw), lambda h, b, i: (b * nq + i, h)),
                  pl.BlockSpec((seq_len, hw), lambda h, b, i: (b, n_heads + h)),
                  pl.BlockSpec((seq_len, hw), lambda h, b, i: (b, 2 * n_heads + h)),
                  pl.BlockSpec((1, t, t), lambda h, b, i: (h, 0, 0)),
                  pl.BlockSpec((1, t, t), lambda h, b, i: (h, 0, 0)),
                  *[vec(a) for a in lams], vec(gs)],
        out_specs=pl.BlockSpec((t, hw), lambda h, b, i: (b * nq + i, h)),
        out_shape=jax.ShapeDtypeStruct((n, vw), F32),
        scratch_shapes=[pltpu.VMEM((seq_len, hw), BF16), pltpu.VMEM((seq_len, hw), BF16)],
        compiler_params=_cparams("parallel", "parallel", "arbitrary"),
        name="attn_prompt",
    )(proj, proj, proj, bias_diag, bias_prev, *lams, gs)


def _attn_sample_kernel(pt_ref, q_ref, kn_ref, vn_ref, *rest, pages, n_heads, hd, t, scale, lam_init):
    kp = rest[:pages]
    vp = rest[pages:2 * pages]
    bt_ref, bn_ref, lq1, lk1, lq2, lk2, gs_ref, o_ref, m_scr, l_scr, acc_scr = rest[2 * pages:]
    del pt_ref
    c = pl.program_id(1)
    hw = 2 * hd
    page_keys = kp[0].shape[0] // n_heads

    @pl.when(c == 0)
    def _():
        m_scr[...] = jnp.full(m_scr.shape, NEG_INF, F32)
        l_scr[...] = jnp.zeros(l_scr.shape, F32)
        acc_scr[...] = jnp.zeros(acc_scr.shape, F32)

    q = q_ref[...] * scale
    qqs = [_two_map_queries(q[:, h * hw:(h + 1) * hw], hd) for h in range(n_heads)]

    def head_rows(refs, h):
        return jnp.concatenate([r[pl.ds(h, page_keys, stride=n_heads), :] for r in refs], axis=0).astype(BF16)

    for h in range(n_heads):
        s = lax.dot_general(qqs[h], head_rows(kp, h), _NT, preferred_element_type=F32) + bt_ref[0, h]
        m, l, acc = _online_softmax_step(s, m_scr[h], l_scr[h], acc_scr[h], head_rows(vp, h))
        m_scr[h] = m
        l_scr[h] = l
        acc_scr[h] = acc

    @pl.when(c == pl.num_programs(1) - 1)
    def _():
        lam = _lam(lq1, lk1, lq2, lk2, lam_init)
        kn = kn_ref[...]
        vn = vn_ref[...]
        pad = jnp.zeros((LANES - t, hw), F32)
        for h in range(n_heads):
            cols = slice(h * hw, (h + 1) * hw)
            knh = jnp.concatenate([kn[:, cols], pad], axis=0).astype(BF16)
            vnh = jnp.concatenate([vn[:, cols], pad], axis=0).astype(BF16)
            s = lax.dot_general(qqs[h], knh, _NT, preferred_element_type=F32) + bn_ref[h]
            _, l, acc = _online_softmax_step(s, m_scr[h], l_scr[h], acc_scr[h], vnh)
            o_ref[:, cols] = _diff_finish(l, acc, t, lam, gs_ref[...], lam_init)


def _attn_sample(page_table, proj, cache_k, cache_v, base, bias_tail, bias_new, lams, gs,
                 *, n_seq, t, n_heads, hd, pages, lam_init):
    hw = 2 * hd
    vw = n_heads * hw
    n_pages = page_table.shape[1]
    assert n_pages % pages == 0
    n_chunks = n_pages // pages
    page_rows = cache_k.shape[1]
    kern = functools.partial(_attn_sample_kernel, pages=pages, n_heads=n_heads, hd=hd, t=t,
                             scale=hd ** -0.5, lam_init=lam_init)

    def page_spec(i):
        return pl.BlockSpec((None, page_rows, hw), lambda b, c, pt: (base + pt[b, c * pages + i], 0, 0))

    vec = lambda a: pl.BlockSpec(a.shape, lambda b, c, pt: (0,) * a.ndim)
    grid_spec = pltpu.PrefetchScalarGridSpec(
        num_scalar_prefetch=1,
        grid=(n_seq, n_chunks),
        in_specs=[pl.BlockSpec((t, vw), lambda b, c, pt: (b, 0)),
                  pl.BlockSpec((t, vw), lambda b, c, pt: (b, 1)),
                  pl.BlockSpec((t, vw), lambda b, c, pt: (b, 2)),
                  *[page_spec(i) for i in range(pages)],
                  *[page_spec(i) for i in range(pages)],
                  pl.BlockSpec((1,) + bias_tail.shape[1:],
                               lambda b, c, pt: (jnp.where(c == n_chunks - 1, 1, 0), 0, 0, 0)),
                  vec(bias_new), *[vec(a) for a in lams], vec(gs)],
        out_specs=pl.BlockSpec((t, vw), lambda b, c, pt: (b, 0)),
        scratch_shapes=[pltpu.VMEM((n_heads, 2 * t, 1), F32),
                        pltpu.VMEM((n_heads, 2 * t, 1), F32),
                        pltpu.VMEM((n_heads, 2 * t, hw), F32)],
    )
    return pl.pallas_call(
        kern,
        grid_spec=grid_spec,
        out_shape=jax.ShapeDtypeStruct((n_seq * t, vw), F32),
        compiler_params=_cparams("parallel", "arbitrary"),
        name="attn_sample",
    )(page_table, proj, proj, proj, *([cache_k] * pages), *([cache_v] * pages),
      bias_tail, bias_new, *lams, gs)


def _post_kernel(x_ref, o_ref, r_ref, ga_ref, gr_ref, p_ref, bm_ref, wo_ref, gf_ref, wfi_ref, wfo_ref,
                 gp_ref, wpg_ref, wpp_ref, gfin_ref, y_ref, *, d_ff, fc, final_norm):
    bm = bm_ref[...]
    mix = (jax.nn.sigmoid(ga_ref[...] + bm[0:1]) * o_ref[...]
           + jax.nn.sigmoid(gr_ref[...] + bm[1:2]) * r_ref[...])
    h = x_ref[...] + jnp.dot(mix.astype(BF16), wo_ref[...], preferred_element_type=F32)
    hn = _rms(h, gf_ref[...]).astype(BF16)
    acc = jnp.zeros_like(h)
    for c in range(d_ff // fc):
        gate = jnp.dot(hn, wfi_ref[:, c * fc:(c + 1) * fc], preferred_element_type=F32)
        up = jnp.dot(hn, wfi_ref[:, d_ff + c * fc:d_ff + (c + 1) * fc], preferred_element_type=F32)
        act = (jax.nn.silu(gate) * up).astype(BF16)
        acc = acc + jnp.dot(act, wfo_ref[c * fc:(c + 1) * fc, :], preferred_element_type=F32)
    h = h + acc
    hp = _rms(h, gp_ref[...]).astype(BF16)
    gate = jax.nn.sigmoid(jnp.dot(hp, wpg_ref[...], preferred_element_type=F32))
    h = h + gate * jnp.dot(p_ref[...].astype(BF16), wpp_ref[...], preferred_element_type=F32)
    y_ref[...] = _rms(h, gfin_ref[...]) if final_norm else h


def _post(x, o, r, proj, col_ga, col_gr, p, bm, wo, gf, wfi, wfo, gp, wpg, wpp, gfin, *, tm, fc, final_norm):
    n, d = x.shape
    d_ff = wfo.shape[0]
    assert d_ff % fc == 0
    kern = functools.partial(_post_kernel, d_ff=d_ff, fc=fc, final_norm=final_norm)
    tok = lambda col: pl.BlockSpec((tm, d), lambda i: (i, col))
    return pl.pallas_call(
        kern,
        grid=(n // tm,),
        in_specs=[tok(0), tok(0), tok(0), tok(col_ga), tok(col_gr),
                  pl.BlockSpec((tm, p.shape[1]), lambda i: (i, 0)),
                  *[_const_spec(a.shape) for a in (bm, wo, gf, wfi, wfo, gp, wpg, wpp, gfin)]],
        out_specs=pl.BlockSpec((tm, d), lambda i: (i, 0)),
        out_shape=jax.ShapeDtypeStruct((n, d), F32),
        compiler_params=_cparams("parallel"),
        name="post",
    )(x, o, r, proj, proj, p, bm, wo, gf, wfi, wfo, gp, wpg, wpp, gfin)


def _t5_bucket(n, n_buckets):
    max_exact = n_buckets // 2
    nf = jnp.maximum(n, 1).astype(F32)
    large = max_exact + (jnp.log(nf / max_exact) / math.log(MAX_DISTANCE / max_exact)
                         * (n_buckets - max_exact)).astype(jnp.int32)
    large = jnp.minimum(large, n_buckets - 1)
    return jnp.where(n < max_exact, n, large)


def _far_distance(n_buckets):
    n = np.arange(1, 4 * MAX_DISTANCE, dtype=np.int64)
    max_exact = n_buckets // 2
    large = max_exact + (np.log(n.astype(np.float32) / max_exact) / math.log(MAX_DISTANCE / max_exact)
                         * (n_buckets - max_exact)).astype(np.int64)
    bucket = np.where(n < max_exact, n, np.minimum(large, n_buckets - 1))
    not_last = np.nonzero(bucket != n_buckets - 1)[0]
    return int(n[not_last[-1]]) + 2


def _shifted_bias(rel_bias, dist):
    nb = rel_bias.shape[0]
    rel = rel_bias.astype(F32) - rel_bias[nb - 1].astype(F32)
    b = rel[_t5_bucket(jnp.maximum(dist, 0), nb)]
    b = jnp.where((dist >= 0)[..., None], b, NEG_INF)
    return jnp.moveaxis(b, -1, 0)


def _gate_weights(w_gx, w_ga):
    nb, bw, _ = w_gx.shape
    per = LANES // bw

    def blockdiag(wb):
        wb = wb.reshape(nb // per, per, bw, bw)
        eye = jnp.eye(per, dtype=wb.dtype)
        return jnp.einsum('gpij,pq->gpiqj', wb, eye).reshape(nb // per, LANES, LANES)

    return jnp.concatenate([blockdiag(w_gx), blockdiag(w_ga)], axis=-1).astype(BF16)


def kernel(x_prompt, x_sample, p_prompt, p_sample, cache_k, cache_v, state_conv, state_rnn, page_table,
           rel_bias, g_mix, w_in, conv_w, conv_b, w_gx, b_gx, w_ga, b_ga, a_param,
           lam_q1, lam_k1, lam_q2, lam_k2, g_subln, b_merge, w_out,
           g_ffn, w_ffn_in, w_ffn_out, g_ple, w_ple_gate, w_ple_proj, g_final):
    bp, sp, d = x_prompt.shape
    bs, ts, _ = x_sample.shape
    depth, n_pool, page_size, n_heads, hw = cache_k.shape
    hd = hw // 2
    n_pages = page_table.shape[1]
    past = n_pages * page_size
    lru_w = conv_w.shape[2]
    kw = conv_w.shape[1]
    assert d == n_heads * hw == lru_w and w_in.shape[2] == 7 * d and hw == LANES
    assert ts == SUBLANES and kw - 1 <= min(ts, SUBLANES) and sp % 256 == 0

    t_attn = 256
    far = _far_distance(rel_bias.shape[0])
    assert t_attn + 1 >= far and page_size + 1 >= far

    r = jnp.arange(t_attn, dtype=jnp.int32)
    diag = r[:, None] - r[None, :]
    bias_diag = _shifted_bias(rel_bias, diag)
    bias_prev = _shifted_bias(rel_bias, diag + t_attn)
    pages = 8
    tq = jnp.arange(ts, dtype=jnp.int32)
    kk = jnp.arange(page_size, dtype=jnp.int32)
    last_page = _shifted_bias(rel_bias, page_size + tq[:, None] - kk[None, :])
    last_page = jnp.concatenate([jnp.zeros((n_heads, ts, (pages - 1) * page_size), F32), last_page], axis=2)
    bias_tail = jnp.stack([jnp.zeros_like(last_page), last_page])
    bias_tail = jnp.concatenate([bias_tail, bias_tail], axis=2)
    kn = jnp.arange(LANES, dtype=jnp.int32)
    new_dist = jnp.where(kn[None, :] < ts, tq[:, None] - kn[None, :], -1)
    bias_new = _shifted_bias(rel_bias, new_dist)
    bias_new = jnp.concatenate([bias_new, bias_new], axis=1)

    ck = cache_k.reshape(depth * n_pool, page_size * n_heads, hw)
    cv = cache_v.reshape(depth * n_pool, page_size * n_heads, hw)

    hp = x_prompt.reshape(bp * sp, d)
    hs = x_sample.reshape(bs * ts, d)
    row = lambda a: a.reshape(1, -1).astype(F32)
    outs = {k: [] for k in ("kp", "vp", "cp", "rp", "ks", "vs", "cs", "rs")}
    for l in range(depth):
        lam_init = 0.8 - 0.6 * math.exp(-0.3 * l)
        w_in_b = w_in[l].astype(BF16)
        wg = _gate_weights(w_gx[l], w_ga[l])
        lams = [row(a[l]) for a in (lam_q1, lam_k1, lam_q2, lam_k2)]
        gs = row(g_subln[l])
        rglru_w = (conv_w[l], row(conv_b[l]), wg, row(b_gx[l]), row(b_ga[l]), row(a_param[l]))
        post_w = (b_merge[l], w_out[l].astype(BF16), row(g_ffn[l]), w_ffn_in[l].astype(BF16),
                  w_ffn_out[l].astype(BF16), row(g_ple[l]), w_ple_gate[l].astype(BF16),
                  w_ple_proj[l].astype(BF16), row(g_final))
        last = l == depth - 1

        proj = _inproj(hp, row(g_mix[l]), w_in_b, tm=512, tn=d)
        r_p, hl_p = _rglru(proj, 3, 4, jnp.zeros((bp, SUBLANES, lru_w), F32), jnp.zeros((bp, 1, lru_w), F32),
                           *rglru_w, n_seq=bp, seq_len=sp, rows=256, reset_first=True)
        o_p = _attn_prompt(proj, bias_diag, bias_prev, lams, gs, n_seq=bp, seq_len=sp, n_heads=n_heads,
                           hd=hd, t=t_attn, lam_init=lam_init)
        hp = _post(hp, o_p, r_p, proj, 5, 6, p_prompt[l].reshape(bp * sp, -1), *post_w,
                   tm=256, fc=256, final_norm=last)
        proj3 = proj.reshape(bp, sp, 7 * d)
        outs["kp"].append(proj3[:, :, d:2 * d].reshape(bp, sp, n_heads, hw))
        outs["vp"].append(proj3[:, :, 2 * d:3 * d].reshape(bp, sp, n_heads, hw))
        outs["cp"].append(proj3[:, sp - (kw - 1):, 3 * d:4 * d])
        outs["rp"].append(hl_p[:, SUBLANES - 1])

        proj = _inproj(hs, row(g_mix[l]), w_in_b, tm=bs * ts, tn=d)
        tail0 = jnp.pad(state_conv[l].astype(F32), ((0, 0), (SUBLANES - (kw - 1), 0), (0, 0)))
        r_s, hl_s = _rglru(proj, 3, 4, tail0, state_rnn[l].astype(F32)[:, None, :],
                           *rglru_w, n_seq=bs, seq_len=ts, rows=ts, reset_first=(past == 0))
        o_s = _attn_sample(page_table, proj, ck, cv, l * n_pool, bias_tail, bias_new, lams, gs,
                           n_seq=bs, t=ts, n_heads=n_heads, hd=hd, pages=pages, lam_init=lam_init)
        hs = _post(hs, o_s, r_s, proj, 5, 6, p_sample[l].reshape(bs * ts, -1), *post_w,
                   tm=bs * ts, fc=256, final_norm=last)
        proj3 = proj.reshape(bs, ts, 7 * d)
        outs["ks"].append(proj3[:, :, d:2 * d].reshape(bs, ts, n_heads, hw))
        outs["vs"].append(proj3[:, :, 2 * d:3 * d].reshape(bs, ts, n_heads, hw))
        outs["cs"].append(proj3[:, ts - (kw - 1):, 3 * d:4 * d])
        outs["rs"].append(hl_s[:, SUBLANES - 1])

    st = lambda k: jnp.stack(outs[k])
    return (hp.reshape(bp, sp, d), hs.reshape(bs, ts, d),
            st("kp"), st("vp"), st("cp"), st("rp"), st("ks"), st("vs"), st("cs"), st("rs"))
```

```python
import functools
import math

import jax
import jax.numpy as jnp
import numpy as np
from jax import lax
from jax.experimental import pallas as pl
from jax.experimental.pallas import tpu as pltpu

F32 = jnp.float32
BF16 = jnp.bfloat16

EPS = 1e-6
NEG_INF = -1e30
MAX_DISTANCE = 128
LRU_C = 8.0
LANES = 128
SUBLANES = 8
VMEM_LIMIT = 56 * 1024 * 1024


def _cparams(*sem):
    return pltpu.CompilerParams(dimension_semantics=sem, vmem_limit_bytes=VMEM_LIMIT)


def _rms(x, g):
    return x * lax.rsqrt(jnp.mean(x * x, axis=-1, keepdims=True) + EPS) * g


def _const_spec(shape):
    nd = len(shape)
    return pl.BlockSpec(shape, lambda *_: (0,) * nd, pipeline_mode=pl.Buffered(1))


def _inproj_kernel(x_ref, g_ref, w_ref, o_ref, xn_ref):
    @pl.when(pl.program_id(1) == 0)
    def _():
        xn_ref[...] = _rms(x_ref[...], g_ref[...]).astype(BF16)

    o_ref[...] = jnp.dot(xn_ref[...], w_ref[...], preferred_element_type=F32)


def _inproj(x, g, w_bf16, tm, tn):
    n, d = x.shape
    nw = w_bf16.shape[1]
    return pl.pallas_call(
        _inproj_kernel,
        grid=(n // tm, nw // tn),
        in_specs=[pl.BlockSpec((tm, d), lambda i, j: (i, 0)),
                  pl.BlockSpec((1, d), lambda i, j: (0, 0)),
                  pl.BlockSpec((d, tn), lambda i, j: (0, j))],
        out_specs=pl.BlockSpec((tm, tn), lambda i, j: (i, j)),
        out_shape=jax.ShapeDtypeStruct((n, nw), F32),
        scratch_shapes=[pltpu.VMEM((tm, d), BF16)],
        compiler_params=_cparams("parallel", "arbitrary"),
        name="inproj",
    )(x, g, w_bf16)


def _rglru_kernel(xr_ref, yr_ref, tail_ref, h0_ref, cw_ref, cb_ref, wg_ref, bgx_ref, bga_ref, ap_ref,
                  r_ref, hlast_ref, tail_scr, carry_scr, *, nseq, rows, reset_first):
    s = pl.program_id(1)
    w = xr_ref.shape[-1]

    @pl.when(s == 0)
    def _():
        tail_scr[...] = tail_ref[...]
        carry_scr[...] = h0_ref[...]

    x = xr_ref[...].reshape(nseq, rows, w)
    tail = tail_scr[...]
    cw = cw_ref[...]
    kw = cw.shape[0]
    row8 = lax.broadcasted_iota(jnp.int32, (1, SUBLANES, w), 1)
    xc = cb_ref[...] + x * cw[kw - 1:kw]
    for k in range(1, kw):
        xs = pltpu.roll(x, k, axis=1)
        head = jnp.where(row8 < k, pltpu.roll(tail, k, axis=1), xs[:, :SUBLANES, :])
        xs = jnp.concatenate([head, xs[:, SUBLANES:, :]], axis=1) if rows > SUBLANES else head
        xc = xc + xs * cw[kw - 1 - k:kw - k]
    tail_scr[...] = x[:, rows - SUBLANES:, :]

    xc2 = xc.reshape(nseq * rows, w)
    xcb = xc2.astype(BF16)
    gx_pre, ga_pre = [], []
    for g in range(w // LANES):
        pre = jnp.dot(xcb[:, g * LANES:(g + 1) * LANES], wg_ref[g], preferred_element_type=F32)
        gx_pre.append(pre[:, :LANES])
        ga_pre.append(pre[:, LANES:])
    gx = jax.nn.sigmoid(jnp.concatenate(gx_pre, axis=1) + bgx_ref[...])
    ga = jax.nn.sigmoid(jnp.concatenate(ga_pre, axis=1) + bga_ref[...])

    z = -ap_ref[...]
    softplus = jnp.maximum(z, 0.0) + jnp.log1p(jnp.exp(-jnp.abs(z)))
    log_a = -LRU_C * ga * softplus
    a = jnp.exp(log_a)
    mult = jnp.sqrt(-jnp.tanh(log_a) * (a * a + 1.0))
    if reset_first:
        t_in_seq = lax.broadcasted_iota(jnp.int32, (nseq * rows, w), 0)
        if nseq > 1:
            t_in_seq = t_in_seq % rows
        first = t_in_seq == jnp.where(s == 0, 0, -1)
        a = jnp.where(first, 0.0, a)
        mult = jnp.where(first, 1.0, mult)
    b = xc2 * gx * mult

    ngrp = nseq * rows // SUBLANES
    av = a.reshape(ngrp, SUBLANES, w)
    bv = b.reshape(ngrp, SUBLANES, w)
    d = 1
    while d < SUBLANES:
        keep = row8 >= d
        a_prev = jnp.where(keep, pltpu.roll(av, d, axis=1), 1.0)
        b_prev = jnp.where(keep, pltpu.roll(bv, d, axis=1), 0.0)
        bv = av * b_prev + bv
        av = av * a_prev
        d *= 2
    av = av.reshape(nseq, rows, w)
    bv = bv.reshape(nseq, rows, w)
    carry = carry_scr[...]
    hs = []
    for g in range(rows // SUBLANES):
        sl = slice(g * SUBLANES, (g + 1) * SUBLANES)
        hg = av[:, sl, :] * carry + bv[:, sl, :]
        carry = hg[:, SUBLANES - 1:, :]
        hs.append(hg)
    carry_scr[...] = carry
    hlast_ref[...] = hs[-1]
    h = jnp.concatenate(hs, axis=1) if len(hs) > 1 else hs[0]
    r_ref[...] = h.reshape(nseq * rows, w) * jax.nn.gelu(yr_ref[...])


def _rglru(proj, col_x, col_y, tail0, h0, cw, cb, wg, bgx, bga, ap, *, n_seq, seq_len, rows, reset_first):
    w = cw.shape[1]
    n = n_seq * seq_len
    if seq_len >= rows:
        assert seq_len % rows == 0
        nseq_t, steps, grid0 = 1, seq_len // rows, n_seq
    else:
        assert seq_len == rows == SUBLANES
        nseq_t, steps, grid0 = n_seq, 1, 1
    tm = nseq_t * rows
    kern = functools.partial(_rglru_kernel, nseq=nseq_t, rows=rows, reset_first=reset_first)
    small = lambda shape: pl.BlockSpec(shape, lambda b, s: (0,) * len(shape))
    return pl.pallas_call(
        kern,
        grid=(grid0, steps),
        in_specs=[pl.BlockSpec((tm, w), lambda b, s: (b * steps + s, col_x)),
                  pl.BlockSpec((tm, w), lambda b, s: (b * steps + s, col_y)),
                  pl.BlockSpec((nseq_t, SUBLANES, w), lambda b, s: (b, 0, 0)),
                  pl.BlockSpec((nseq_t, 1, w), lambda b, s: (b, 0, 0)),
                  small(cw.shape), small(cb.shape), small(wg.shape),
                  small(bgx.shape), small(bga.shape), small(ap.shape)],
        out_specs=[pl.BlockSpec((tm, w), lambda b, s: (b * steps + s, 0)),
                   pl.BlockSpec((nseq_t, SUBLANES, w), lambda b, s: (b, 0, 0))],
        out_shape=[jax.ShapeDtypeStruct((n, w), F32),
                   jax.ShapeDtypeStruct((n_seq, SUBLANES, w), F32)],
        scratch_shapes=[pltpu.VMEM((nseq_t, SUBLANES, w), F32),
                        pltpu.VMEM((nseq_t, 1, w), F32)],
        compiler_params=_cparams("parallel", "arbitrary"),
        name="rglru",
    )(proj, proj, tail0, h0, cw, cb, wg, bgx, bga, ap)


def _lam(lq1, lk1, lq2, lk2, lam_init):
    return (jnp.exp(jnp.sum(lq1[...] * lk1[...], axis=-1, keepdims=True))
            - jnp.exp(jnp.sum(lq2[...] * lk2[...], axis=-1, keepdims=True)) + lam_init)


def _two_map_queries(q, hd):
    lane = lax.broadcasted_iota(jnp.int32, q.shape, 1)
    return jnp.concatenate([jnp.where(lane < hd, q, 0.0), jnp.where(lane >= hd, q, 0.0)], axis=0).astype(BF16)


def _online_softmax_step(s, m, l, acc, v):
    m_new = jnp.maximum(m, jnp.max(s, axis=1, keepdims=True))
    alpha = jnp.exp(m - m_new)
    p = jnp.exp(s - m_new)
    l = alpha * l + jnp.sum(p, axis=1, keepdims=True)
    acc = alpha * acc + jnp.dot(p.astype(BF16), v, preferred_element_type=F32)
    return m_new, l, acc


def _diff_finish(l, acc, t, lam, g, lam_init):
    o = acc / l
    o = o[:t] - lam * o[t:]
    return _rms(o, g) * (1.0 - lam_init)


_NT = (((1,), (1,)), ((), ()))


def _attn_prompt_kernel(q_ref, k_ref, v_ref, bd_ref, bp_ref, lq1, lk1, lq2, lk2, gs_ref, o_ref,
                        kb_ref, vb_ref, *, t, hd, hps, scale, lam_init):
    qi = pl.program_id(2)
    hw = 2 * hd

    @pl.when(qi == 0)
    def _():
        kb_ref[...] = k_ref[...].astype(BF16)
        vb_ref[...] = v_ref[...].astype(BF16)

    q = q_ref[...] * scale
    qqs = [_two_map_queries(q[:, g * hw:(g + 1) * hw], hd) for g in range(hps)]

    def chunk(j, carries, bias_ref):
        start = pl.multiple_of(j * t, t)
        out = []
        for g in range(hps):
            cols = slice(g * hw, (g + 1) * hw)
            s = lax.dot_general(qqs[g], kb_ref[pl.ds(start, t), cols], _NT, preferred_element_type=F32)
            if bias_ref is not None:
                s = s + jnp.concatenate([bias_ref[g], bias_ref[g]], axis=0)
            out.append(_online_softmax_step(s, *carries[g], vb_ref[pl.ds(start, t), cols]))
        return tuple(out)

    init = (jnp.full((2 * t, 1), NEG_INF, F32), jnp.zeros((2 * t, 1), F32), jnp.zeros((2 * t, hw), F32))
    carries = (init,) * hps
    carries = lax.fori_loop(0, jnp.maximum(qi - 1, 0), lambda j, c: chunk(j, c, None), carries)
    carries = lax.cond(qi >= 1, lambda c: chunk(qi - 1, c, bp_ref), lambda c: c, carries)
    carries = chunk(qi, carries, bd_ref)
    lam = _lam(lq1, lk1, lq2, lk2, lam_init)
    for g in range(hps):
        _, l, acc = carries[g]
        o_ref[:, g * hw:(g + 1) * hw] = _diff_finish(l, acc, t, lam, gs_ref[...], lam_init)


def _attn_prompt(proj, bias_diag, bias_prev, lams, gs, *, n_seq, seq_len, n_heads, hd, t, hps, lam_init):
    n, vw = n_seq * seq_len, n_heads * 2 * hd
    nq = seq_len // t
    bw = hps * 2 * hd
    ng = n_heads // hps
    kern = functools.partial(_attn_prompt_kernel, t=t, hd=hd, hps=hps, scale=hd ** -0.5, lam_init=lam_init)
    vec = lambda a: pl.BlockSpec(a.shape, lambda h, b, i: (0, 0))
    return pl.pallas_call(
        kern,
        grid=(ng, n_seq, nq),
        in_specs=[pl.BlockSpec((t, bw), lambda h, b, i: (b * nq + i, h)),
                  pl.BlockSpec((seq_len, bw), lambda h, b, i: (b, ng + h)),
                  pl.BlockSpec((seq_len, bw), lambda h, b, i: (b, 2 * ng + h)),
                  pl.BlockSpec((hps, t, t), lambda h, b, i: (h, 0, 0)),
                  pl.BlockSpec((hps, t, t), lambda h, b, i: (h, 0, 0)),
                  *[vec(a) for a in lams], vec(gs)],
        out_specs=pl.BlockSpec((t, bw), lambda h, b, i: (b * nq + i, h)),
        out_shape=jax.ShapeDtypeStruct((n, vw), F32),
        scratch_shapes=[pltpu.VMEM((seq_len, bw), BF16), pltpu.VMEM((seq_len, bw), BF16)],
        compiler_params=_cparams("parallel", "parallel", "arbitrary"),
        name="attn_prompt",
    )(proj, proj, proj, bias_diag, bias_prev, *lams, gs)


def _attn_sample_kernel(pt_ref, q_ref, kn_ref, vn_ref, *rest, pages, n_heads, hd, t, scale, lam_init):
    kp = rest[:pages]
    vp = rest[pages:2 * pages]
    bt_ref, bn_ref, lq1, lk1, lq2, lk2, gs_ref, o_ref, m_scr, l_scr, acc_scr = rest[2 * pages:]
    del pt_ref
    c = pl.program_id(1)
    hw = 2 * hd
    page_keys = kp[0].shape[0] // n_heads

    @pl.when(c == 0)
    def _():
        m_scr[...] = jnp.full(m_scr.shape, NEG_INF, F32)
        l_scr[...] = jnp.zeros(l_scr.shape, F32)
        acc_scr[...] = jnp.zeros(acc_scr.shape, F32)

    q = q_ref[...] * scale
    qqs = [_two_map_queries(q[:, h * hw:(h + 1) * hw], hd) for h in range(n_heads)]

    def head_rows(refs, h):
        return jnp.concatenate([r[pl.ds(h, page_keys, stride=n_heads), :] for r in refs], axis=0).astype(BF16)

    for h in range(n_heads):
        s = lax.dot_general(qqs[h], head_rows(kp, h), _NT, preferred_element_type=F32) + bt_ref[0, h]
        m, l, acc = _online_softmax_step(s, m_scr[h], l_scr[h], acc_scr[h], head_rows(vp, h))
        m_scr[h] = m
        l_scr[h] = l
        acc_scr[h] = acc

    @pl.when(c == pl.num_programs(1) - 1)
    def _():
        lam = _lam(lq1, lk1, lq2, lk2, lam_init)
        kn = kn_ref[...]
        vn = vn_ref[...]
        pad = jnp.zeros((LANES - t, hw), F32)
        for h in range(n_heads):
            cols = slice(h * hw, (h + 1) * hw)
            knh = jnp.concatenate([kn[:, cols], pad], axis=0).astype(BF16)
            vnh = jnp.concatenate([vn[:, cols], pad], axis=0).astype(BF16)
            s = lax.dot_general(qqs[h], knh, _NT, preferred_element_type=F32) + bn_ref[h]
            _, l, acc = _online_softmax_step(s, m_scr[h], l_scr[h], acc_scr[h], vnh)
            o_ref[:, cols] = _diff_finish(l, acc, t, lam, gs_ref[...], lam_init)


def _attn_sample(page_table, proj, cache_k, cache_v, base, bias_tail, bias_new, lams, gs,
                 *, n_seq, t, n_heads, hd, pages, lam_init):
    hw = 2 * hd
    vw = n_heads * hw
    n_pages = page_table.shape[1]
    assert n_pages % pages == 0
    n_chunks = n_pages // pages
    page_rows = cache_k.shape[1]
    kern = functools.partial(_attn_sample_kernel, pages=pages, n_heads=n_heads, hd=hd, t=t,
                             scale=hd ** -0.5, lam_init=lam_init)

    def page_spec(i):
        return pl.BlockSpec((None, page_rows, hw), lambda b, c, pt: (base + pt[b, c * pages + i], 0, 0))

    vec = lambda a: pl.BlockSpec(a.shape, lambda b, c, pt: (0,) * a.ndim)
    grid_spec = pltpu.PrefetchScalarGridSpec(
        num_scalar_prefetch=1,
        grid=(n_seq, n_chunks),
        in_specs=[pl.BlockSpec((t, vw), lambda b, c, pt: (b, 0)),
                  pl.BlockSpec((t, vw), lambda b, c, pt: (b, 1)),
                  pl.BlockSpec((t, vw), lambda b, c, pt: (b, 2)),
                  *[page_spec(i) for i in range(pages)],
                  *[page_spec(i) for i in range(pages)],
                  pl.BlockSpec((1,) + bias_tail.shape[1:],
                               lambda b, c, pt: (jnp.where(c == n_chunks - 1, 1, 0), 0, 0, 0)),
                  vec(bias_new), *[vec(a) for a in lams], vec(gs)],
        out_specs=pl.BlockSpec((t, vw), lambda b, c, pt: (b, 0)),
        scratch_shapes=[pltpu.VMEM((n_heads, 2 * t, 1), F32),
                        pltpu.VMEM((n_heads, 2 * t, 1), F32),
                        pltpu.VMEM((n_heads, 2 * t, hw), F32)],
    )
    return pl.pallas_call(
        kern,
        grid_spec=grid_spec,
        out_shape=jax.ShapeDtypeStruct((n_seq * t, vw), F32),
        compiler_params=_cparams("parallel", "arbitrary"),
        name="attn_sample",
    )(page_table, proj, proj, proj, *([cache_k] * pages), *([cache_v] * pages),
      bias_tail, bias_new, *lams, gs)


def _post_kernel(x_ref, o_ref, r_ref, ga_ref, gr_ref, p_ref, bm_ref, wo_ref, gf_ref, wfi_ref, wfo_ref,
                 gp_ref, wpg_ref, wpp_ref, gfin_ref, y_ref, *, d_ff, fc, final_norm):
    bm = bm_ref[...]
    mix = (jax.nn.sigmoid(ga_ref[...] + bm[0:1]) * o_ref[...]
           + jax.nn.sigmoid(gr_ref[...] + bm[1:2]) * r_ref[...])
    h = x_ref[...] + jnp.dot(mix.astype(BF16), wo_ref[...], preferred_element_type=F32)
    hn = _rms(h, gf_ref[...]).astype(BF16)
    acc = jnp.zeros_like(h)
    for c in range(d_ff // fc):
        gate = jnp.dot(hn, wfi_ref[:, c * fc:(c + 1) * fc], preferred_element_type=F32)
        up = jnp.dot(hn, wfi_ref[:, d_ff + c * fc:d_ff + (c + 1) * fc], preferred_element_type=F32)
        act = (jax.nn.silu(gate) * up).astype(BF16)
        acc = acc + jnp.dot(act, wfo_ref[c * fc:(c + 1) * fc, :], preferred_element_type=F32)
    h = h + acc
    hp = _rms(h, gp_ref[...]).astype(BF16)
    gate = jax.nn.sigmoid(jnp.dot(hp, wpg_ref[...], preferred_element_type=F32))
    h = h + gate * jnp.dot(p_ref[...].astype(BF16), wpp_ref[...], preferred_element_type=F32)
    y_ref[...] = _rms(h, gfin_ref[...]) if final_norm else h


def _post(x, o, r, proj, col_ga, col_gr, p, bm, wo, gf, wfi, wfo, gp, wpg, wpp, gfin, *, tm, fc, final_norm):
    n, d = x.shape
    d_ff = wfo.shape[0]
    assert d_ff % fc == 0
    kern = functools.partial(_post_kernel, d_ff=d_ff, fc=fc, final_norm=final_norm)
    tok = lambda col: pl.BlockSpec((tm, d), lambda i: (i, col))
    return pl.pallas_call(
        kern,
        grid=(n // tm,),
        in_specs=[tok(0), tok(0), tok(0), tok(col_ga), tok(col_gr),
                  pl.BlockSpec((tm, p.shape[1]), lambda i: (i, 0)),
                  *[_const_spec(a.shape) for a in (bm, wo, gf, wfi, wfo, gp, wpg, wpp, gfin)]],
        out_specs=pl.BlockSpec((tm, d), lambda i: (i, 0)),
        out_shape=jax.ShapeDtypeStruct((n, d), F32),
        compiler_params=_cparams("parallel"),
        name="post",
    )(x, o, r, proj, proj, p, bm, wo, gf, wfi, wfo, gp, wpg, wpp, gfin)


def _t5_bucket(n, n_buckets):
    max_exact = n_buckets // 2
    nf = jnp.maximum(n, 1).astype(F32)
    large = max_exact + (jnp.log(nf / max_exact) / math.log(MAX_DISTANCE / max_exact)
                         * (n_buckets - max_exact)).astype(jnp.int32)
    large = jnp.minimum(large, n_buckets - 1)
    return jnp.where(n < max_exact, n, large)


def _far_distance(n_buckets):
    n = np.arange(1, 4 * MAX_DISTANCE, dtype=np.int64)
    max_exact = n_buckets // 2
    large = max_exact + (np.log(n.astype(np.float32) / max_exact) / math.log(MAX_DISTANCE / max_exact)
                         * (n_buckets - max_exact)).astype(np.int64)
    bucket = np.where(n < max_exact, n, np.minimum(large, n_buckets - 1))
    not_last = np.nonzero(bucket != n_buckets - 1)[0]
    return int(n[not_last[-1]]) + 2


def _shifted_bias(rel_bias, dist):
    nb, nh = rel_bias.shape
    rel = rel_bias.astype(F32) - rel_bias[nb - 1].astype(F32)
    bucket = _t5_bucket(jnp.maximum(dist, 0), nb)
    b = jnp.zeros((nh,) + dist.shape, F32)
    for k in range(nb - 1):
        b = jnp.where(bucket == k, rel[k].reshape((nh,) + (1,) * dist.ndim), b)
    return jnp.where(dist >= 0, b, NEG_INF)


def _gate_weights(w_gx, w_ga):
    nb, bw, _ = w_gx.shape
    per = LANES // bw

    def blockdiag(wb):
        wb = wb.reshape(nb // per, per, bw, bw)
        eye = jnp.eye(per, dtype=wb.dtype)
        return jnp.einsum('gpij,pq->gpiqj', wb, eye).reshape(nb // per, LANES, LANES)

    return jnp.concatenate([blockdiag(w_gx), blockdiag(w_ga)], axis=-1).astype(BF16)


def kernel(x_prompt, x_sample, p_prompt, p_sample, cache_k, cache_v, state_conv, state_rnn, page_table,
           rel_bias, g_mix, w_in, conv_w, conv_b, w_gx, b_gx, w_ga, b_ga, a_param,
           lam_q1, lam_k1, lam_q2, lam_k2, g_subln, b_merge, w_out,
           g_ffn, w_ffn_in, w_ffn_out, g_ple, w_ple_gate, w_ple_proj, g_final):
    bp, sp, d = x_prompt.shape
    bs, ts, _ = x_sample.shape
    depth, n_pool, page_size, n_heads, hw = cache_k.shape
    hd = hw // 2
    n_pages = page_table.shape[1]
    past = n_pages * page_size
    lru_w = conv_w.shape[2]
    kw = conv_w.shape[1]
    assert d == n_heads * hw == lru_w and w_in.shape[2] == 7 * d and hw == LANES
    assert ts == SUBLANES and kw - 1 <= min(ts, SUBLANES) and sp % 256 == 0

    t_attn = 512
    far = _far_distance(rel_bias.shape[0])
    assert t_attn + 1 >= far and page_size + 1 >= far

    r = jnp.arange(t_attn, dtype=jnp.int32)
    diag = r[:, None] - r[None, :]
    bias_diag = _shifted_bias(rel_bias, diag)
    bias_prev = _shifted_bias(rel_bias, diag + t_attn)
    pages = 8
    tq = jnp.arange(ts, dtype=jnp.int32)
    kk = jnp.arange(page_size, dtype=jnp.int32)
    last_page = _shifted_bias(rel_bias, page_size + tq[:, None] - kk[None, :])
    last_page = jnp.concatenate([jnp.zeros((n_heads, ts, (pages - 1) * page_size), F32), last_page], axis=2)
    bias_tail = jnp.stack([jnp.zeros_like(last_page), last_page])
    bias_tail = jnp.concatenate([bias_tail, bias_tail], axis=2)
    kn = jnp.arange(LANES, dtype=jnp.int32)
    new_dist = jnp.where(kn[None, :] < ts, tq[:, None] - kn[None, :], -1)
    bias_new = _shifted_bias(rel_bias, new_dist)
    bias_new = jnp.concatenate([bias_new, bias_new], axis=1)

    ck = cache_k.reshape(depth * n_pool, page_size * n_heads, hw)
    cv = cache_v.reshape(depth * n_pool, page_size * n_heads, hw)

    hp = x_prompt.reshape(bp * sp, d)
    hs = x_sample.reshape(bs * ts, d)
    row = lambda a: a.reshape(1, -1).astype(F32)
    outs = {k: [] for k in ("kp", "vp", "cp", "rp", "ks", "vs", "cs", "rs")}
    for l in range(depth):
        lam_init = 0.8 - 0.6 * math.exp(-0.3 * l)
        w_in_b = w_in[l].astype(BF16)
        wg = _gate_weights(w_gx[l], w_ga[l])
        lams = [row(a[l]) for a in (lam_q1, lam_k1, lam_q2, lam_k2)]
        gs = row(g_subln[l])
        rglru_w = (conv_w[l], row(conv_b[l]), wg, row(b_gx[l]), row(b_ga[l]), row(a_param[l]))
        post_w = (b_merge[l], w_out[l].astype(BF16), row(g_ffn[l]), w_ffn_in[l].astype(BF16),
                  w_ffn_out[l].astype(BF16), row(g_ple[l]), w_ple_gate[l].astype(BF16),
                  w_ple_proj[l].astype(BF16), row(g_final))
        last = l == depth - 1

        proj = _inproj(hp, row(g_mix[l]), w_in_b, tm=512, tn=d)
        r_p, hl_p = _rglru(proj, 3, 4, jnp.zeros((bp, SUBLANES, lru_w), F32), jnp.zeros((bp, 1, lru_w), F32),
                           *rglru_w, n_seq=bp, seq_len=sp, rows=256, reset_first=True)
        o_p = _attn_prompt(proj, bias_diag, bias_prev, lams, gs, n_seq=bp, seq_len=sp, n_heads=n_heads,
                           hd=hd, t=t_attn, hps=2, lam_init=lam_init)
        hp = _post(hp, o_p, r_p, proj, 5, 6, p_prompt[l].reshape(bp * sp, -1), *post_w,
                   tm=256, fc=256, final_norm=last)
        proj3 = proj.reshape(bp, sp, 7 * d)
        outs["kp"].append(proj3[:, :, d:2 * d].reshape(bp, sp, n_heads, hw))
        outs["vp"].append(proj3[:, :, 2 * d:3 * d].reshape(bp, sp, n_heads, hw))
        outs["cp"].append(proj3[:, sp - (kw - 1):, 3 * d:4 * d])
        outs["rp"].append(hl_p[:, SUBLANES - 1])

        proj = _inproj(hs, row(g_mix[l]), w_in_b, tm=bs * ts, tn=d)
        tail0 = jnp.pad(state_conv[l].astype(F32), ((0, 0), (SUBLANES - (kw - 1), 0), (0, 0)))
        r_s, hl_s = _rglru(proj, 3, 4, tail0, state_rnn[l].astype(F32)[:, None, :],
                           *rglru_w, n_seq=bs, seq_len=ts, rows=ts, reset_first=(past == 0))
        o_s = _attn_sample(page_table, proj, ck, cv, l * n_pool, bias_tail, bias_new, lams, gs,
                           n_seq=bs, t=ts, n_heads=n_heads, hd=hd, pages=pages, lam_init=lam_init)
        hs = _post(hs, o_s, r_s, proj, 5, 6, p_sample[l].reshape(bs * ts, -1), *post_w,
                   tm=bs * ts, fc=256, final_norm=last)
        proj3 = proj.reshape(bs, ts, 7 * d)
        outs["ks"].append(proj3[:, :, d:2 * d].reshape(bs, ts, n_heads, hw))
        outs["vs"].append(proj3[:, :, 2 * d:3 * d].reshape(bs, ts, n_heads, hw))
        outs["cs"].append(proj3[:, ts - (kw - 1):, 3 * d:4 * d])
        outs["rs"].append(hl_s[:, SUBLANES - 1])

    st = lambda k: jnp.stack(outs[k])
    return (hp.reshape(bp, sp, d), hs.reshape(bs, ts, d),
            st("kp"), st("vp"), st("cp"), st("rp"), st("ks"), st("vs"), st("cs"), st("rs"))
```

```python
import functools
import math

import jax
import jax.numpy as jnp
import numpy as np
from jax import lax
from jax.experimental import pallas as pl
from jax.experimental.pallas import tpu as pltpu

F32 = jnp.float32
BF16 = jnp.bfloat16

EPS = 1e-6
NEG_INF = -1e30
MAX_DISTANCE = 128
LRU_C = 8.0
LOG2E = math.log2(math.e)
LANES = 128
SUBLANES = 8
VMEM_LIMIT = 56 * 1024 * 1024


def _cparams(*sem):
    return pltpu.CompilerParams(dimension_semantics=sem, vmem_limit_bytes=VMEM_LIMIT)


def _rms(x, g):
    return x * lax.rsqrt(jnp.mean(x * x, axis=-1, keepdims=True) + EPS) * g


def _const_spec(shape):
    nd = len(shape)
    return pl.BlockSpec(shape, lambda *_: (0,) * nd, pipeline_mode=pl.Buffered(1))


def _inproj_kernel(x_ref, g_ref, w_ref, o_ref, xn_ref):
    @pl.when(pl.program_id(1) == 0)
    def _():
        xn_ref[...] = _rms(x_ref[...], g_ref[...]).astype(BF16)

    o_ref[...] = jnp.dot(xn_ref[...], w_ref[...], preferred_element_type=F32)


def _inproj(x, g, w_bf16, tm, tn):
    n, d = x.shape
    nw = w_bf16.shape[1]
    return pl.pallas_call(
        _inproj_kernel,
        grid=(n // tm, nw // tn),
        in_specs=[pl.BlockSpec((tm, d), lambda i, j: (i, 0)),
                  pl.BlockSpec((1, d), lambda i, j: (0, 0)),
                  pl.BlockSpec((d, tn), lambda i, j: (0, j))],
        out_specs=pl.BlockSpec((tm, tn), lambda i, j: (i, j)),
        out_shape=jax.ShapeDtypeStruct((n, nw), F32),
        scratch_shapes=[pltpu.VMEM((tm, d), BF16)],
        compiler_params=_cparams("parallel", "arbitrary"),
        name="inproj",
    )(x, g, w_bf16)


def _rglru_kernel(xr_ref, yr_ref, tail_ref, h0_ref, cw_ref, cb_ref, wg_ref, bgx_ref, bga_ref, ap_ref,
                  r_ref, hlast_ref, tail_scr, carry_scr, *, nseq, rows, reset_first):
    s = pl.program_id(1)
    w = xr_ref.shape[-1]

    @pl.when(s == 0)
    def _():
        tail_scr[...] = tail_ref[...]
        carry_scr[...] = h0_ref[...]

    x = xr_ref[...].reshape(nseq, rows, w)
    tail = tail_scr[...]
    cw = cw_ref[...]
    kw = cw.shape[0]
    row8 = lax.broadcasted_iota(jnp.int32, (1, SUBLANES, w), 1)
    xc = cb_ref[...] + x * cw[kw - 1:kw]
    for k in range(1, kw):
        xs = pltpu.roll(x, k, axis=1)
        head = jnp.where(row8 < k, pltpu.roll(tail, k, axis=1), xs[:, :SUBLANES, :])
        xs = jnp.concatenate([head, xs[:, SUBLANES:, :]], axis=1) if rows > SUBLANES else head
        xc = xc + xs * cw[kw - 1 - k:kw - k]
    tail_scr[...] = x[:, rows - SUBLANES:, :]

    xc2 = xc.reshape(nseq * rows, w)
    xcb = xc2.astype(BF16)
    gx_pre, ga_pre = [], []
    for g in range(w // LANES):
        pre = jnp.dot(xcb[:, g * LANES:(g + 1) * LANES], wg_ref[g], preferred_element_type=F32)
        gx_pre.append(pre[:, :LANES])
        ga_pre.append(pre[:, LANES:])
    gx = jax.nn.sigmoid(jnp.concatenate(gx_pre, axis=1) + bgx_ref[...])
    ga = jax.nn.sigmoid(jnp.concatenate(ga_pre, axis=1) + bga_ref[...])

    z = -ap_ref[...]
    softplus = jnp.maximum(z, 0.0) + jnp.log1p(jnp.exp(-jnp.abs(z)))
    log_a = -LRU_C * ga * softplus
    a = jnp.exp(log_a)
    mult = jnp.sqrt(-jnp.tanh(log_a) * (a * a + 1.0))
    if reset_first:
        t_in_seq = lax.broadcasted_iota(jnp.int32, (nseq * rows, w), 0)
        if nseq > 1:
            t_in_seq = t_in_seq % rows
        first = t_in_seq == jnp.where(s == 0, 0, -1)
        a = jnp.where(first, 0.0, a)
        mult = jnp.where(first, 1.0, mult)
    b = xc2 * gx * mult

    ngrp = nseq * rows // SUBLANES
    av = a.reshape(ngrp, SUBLANES, w)
    bv = b.reshape(ngrp, SUBLANES, w)
    d = 1
    while d < SUBLANES:
        keep = row8 >= d
        a_prev = jnp.where(keep, pltpu.roll(av, d, axis=1), 1.0)
        b_prev = jnp.where(keep, pltpu.roll(bv, d, axis=1), 0.0)
        bv = av * b_prev + bv
        av = av * a_prev
        d *= 2
    av = av.reshape(nseq, rows, w)
    bv = bv.reshape(nseq, rows, w)
    carry = carry_scr[...]
    hs = []
    for g in range(rows // SUBLANES):
        sl = slice(g * SUBLANES, (g + 1) * SUBLANES)
        hg = av[:, sl, :] * carry + bv[:, sl, :]
        carry = hg[:, SUBLANES - 1:, :]
        hs.append(hg)
    carry_scr[...] = carry
    hlast_ref[...] = hs[-1]
    h = jnp.concatenate(hs, axis=1) if len(hs) > 1 else hs[0]
    r_ref[...] = h.reshape(nseq * rows, w) * jax.nn.gelu(yr_ref[...])


def _rglru(proj, col_x, col_y, tail0, h0, cw, cb, wg, bgx, bga, ap, *, n_seq, seq_len, rows, reset_first):
    w = cw.shape[1]
    n = n_seq * seq_len
    if seq_len >= rows:
        assert seq_len % rows == 0
        nseq_t, steps, grid0 = 1, seq_len // rows, n_seq
    else:
        assert seq_len == rows == SUBLANES
        nseq_t, steps, grid0 = n_seq, 1, 1
    tm = nseq_t * rows
    kern = functools.partial(_rglru_kernel, nseq=nseq_t, rows=rows, reset_first=reset_first)
    small = lambda shape: pl.BlockSpec(shape, lambda b, s: (0,) * len(shape))
    return pl.pallas_call(
        kern,
        grid=(grid0, steps),
        in_specs=[pl.BlockSpec((tm, w), lambda b, s: (b * steps + s, col_x)),
                  pl.BlockSpec((tm, w), lambda b, s: (b * steps + s, col_y)),
                  pl.BlockSpec((nseq_t, SUBLANES, w), lambda b, s: (b, 0, 0)),
                  pl.BlockSpec((nseq_t, 1, w), lambda b, s: (b, 0, 0)),
                  small(cw.shape), small(cb.shape), small(wg.shape),
                  small(bgx.shape), small(bga.shape), small(ap.shape)],
        out_specs=[pl.BlockSpec((tm, w), lambda b, s: (b * steps + s, 0)),
                   pl.BlockSpec((nseq_t, SUBLANES, w), lambda b, s: (b, 0, 0))],
        out_shape=[jax.ShapeDtypeStruct((n, w), F32),
                   jax.ShapeDtypeStruct((n_seq, SUBLANES, w), F32)],
        scratch_shapes=[pltpu.VMEM((nseq_t, SUBLANES, w), F32),
                        pltpu.VMEM((nseq_t, 1, w), F32)],
        compiler_params=_cparams("parallel", "arbitrary"),
        name="rglru",
    )(proj, proj, tail0, h0, cw, cb, wg, bgx, bga, ap)


def _lam(lq1, lk1, lq2, lk2, lam_init):
    return (jnp.exp(jnp.sum(lq1[...] * lk1[...], axis=-1, keepdims=True))
            - jnp.exp(jnp.sum(lq2[...] * lk2[...], axis=-1, keepdims=True)) + lam_init)


def _two_map_queries(q, hd):
    lane = lax.broadcasted_iota(jnp.int32, q.shape, 1)
    return jnp.concatenate([jnp.where(lane < hd, q, 0.0), jnp.where(lane >= hd, q, 0.0)], axis=0).astype(BF16)


def _softmax_numerators(s, m):
    m_new = jnp.maximum(m, jnp.max(s, axis=1, keepdims=True))
    return m_new, jnp.exp2(m - m_new), jnp.exp2(s - m_new)


def _online_softmax_step(s, m, l, acc, v):
    m_new, alpha, p = _softmax_numerators(s, m)
    l = alpha * l + jnp.sum(p, axis=1, keepdims=True)
    acc = alpha * acc + jnp.dot(p.astype(BF16), v, preferred_element_type=F32)
    return m_new, l, acc


def _diff_finish(o, t, lam, g, lam_init):
    return _rms(o[:t] - lam * o[t:], g) * (1.0 - lam_init)


_NT = (((1,), (1,)), ((), ()))


def _attn_prompt_kernel(q_ref, k_ref, v_ref, bd_ref, bp_ref, lq1, lk1, lq2, lk2, gs_ref, o_ref,
                        kb_ref, vb_ref, *, t, hd, hps, scale, lam_init):
    qi = pl.program_id(2)
    hw = 2 * hd

    @pl.when(qi == 0)
    def _():
        kb_ref[...] = k_ref[...].astype(BF16)
        for g in range(hps):
            vb_ref[:, 2 * g * hw:(2 * g + 1) * hw] = v_ref[:, g * hw:(g + 1) * hw].astype(BF16)
            vb_ref[:, (2 * g + 1) * hw:(2 * g + 2) * hw] = jnp.ones((vb_ref.shape[0], hw), BF16)

    q = q_ref[...] * scale
    lane = lax.broadcasted_iota(jnp.int32, (t, hw), 1)
    chains = []
    for g in range(hps):
        qg = q[:, g * hw:(g + 1) * hw]
        chains.append((g, jnp.where(lane < hd, qg, 0.0).astype(BF16)))
        chains.append((g, jnp.where(lane >= hd, qg, 0.0).astype(BF16)))

    def chunk(j, state, bias_ref):
        start = pl.multiple_of(j * t, t)
        out = []
        for (g, qz), (m, acc) in zip(chains, state):
            s = lax.dot_general(qz, kb_ref[pl.ds(start, t), g * hw:(g + 1) * hw], _NT, preferred_element_type=F32)
            if bias_ref is not None:
                s = s + bias_ref[g]
            m, alpha, p = _softmax_numerators(s, m)
            acc = alpha * acc + jnp.dot(p.astype(BF16), vb_ref[pl.ds(start, t), 2 * g * hw:(2 * g + 2) * hw],
                                        preferred_element_type=F32)
            out.append((m, acc))
        return tuple(out)

    state = ((jnp.full((t, 1), NEG_INF, F32), jnp.zeros((t, 2 * hw), F32)),) * len(chains)
    state = lax.fori_loop(0, jnp.maximum(qi - 1, 0), lambda j, c: chunk(j, c, None), state)
    state = lax.cond(qi >= 1, lambda c: chunk(qi - 1, c, bp_ref), lambda c: c, state)
    state = chunk(qi, state, bd_ref)
    lam = _lam(lq1, lk1, lq2, lk2, lam_init)
    for g in range(hps):
        o = jnp.concatenate([state[2 * g + c][1][:, :hw] / state[2 * g + c][1][:, hw:] for c in range(2)], axis=0)
        o_ref[:, g * hw:(g + 1) * hw] = _diff_finish(o, t, lam, gs_ref[...], lam_init)


def _attn_prompt(proj, bias_diag, bias_prev, lams, gs, *, n_seq, seq_len, n_heads, hd, t, hps, lam_init):
    n, vw = n_seq * seq_len, n_heads * 2 * hd
    nq = seq_len // t
    bw = hps * 2 * hd
    ng = n_heads // hps
    kern = functools.partial(_attn_prompt_kernel, t=t, hd=hd, hps=hps, scale=hd ** -0.5 * LOG2E, lam_init=lam_init)
    vec = lambda a: pl.BlockSpec(a.shape, lambda h, b, i: (0, 0))
    return pl.pallas_call(
        kern,
        grid=(ng, n_seq, nq),
        in_specs=[pl.BlockSpec((t, bw), lambda h, b, i: (b * nq + i, h)),
                  pl.BlockSpec((seq_len, bw), lambda h, b, i: (b, ng + h)),
                  pl.BlockSpec((seq_len, bw), lambda h, b, i: (b, 2 * ng + h)),
                  pl.BlockSpec((hps, t, t), lambda h, b, i: (h, 0, 0)),
                  pl.BlockSpec((hps, t, t), lambda h, b, i: (h, 0, 0)),
                  *[vec(a) for a in lams], vec(gs)],
        out_specs=pl.BlockSpec((t, bw), lambda h, b, i: (b * nq + i, h)),
        out_shape=jax.ShapeDtypeStruct((n, vw), F32),
        scratch_shapes=[pltpu.VMEM((seq_len, bw), BF16), pltpu.VMEM((seq_len, 2 * bw), BF16)],
        compiler_params=_cparams("parallel", "parallel", "arbitrary"),
        name="attn_prompt",
    )(proj, proj, proj, bias_diag, bias_prev, *lams, gs)


def _attn_sample_kernel(pt_ref, q_ref, kn_ref, vn_ref, *rest, pages, n_heads, hd, t, scale, lam_init):
    kp = rest[:pages]
    vp = rest[pages:2 * pages]
    mask_ref, mlast_ref, bn_ref, lq1, lk1, lq2, lk2, gs_ref, o_ref, qr_scr, m_scr, l_scr, acc_scr = rest[2 * pages:]
    del pt_ref
    c = pl.program_id(1)
    hw = 2 * hd

    @pl.when(c == 0)
    def _():
        q = q_ref[...] * scale
        qr_scr[...] = jnp.concatenate(
            [_two_map_queries(q[:, h * hw:(h + 1) * hw], hd) for h in range(n_heads)], axis=0)
        m_scr[...] = jnp.full(m_scr.shape, NEG_INF, F32)
        l_scr[...] = jnp.zeros(l_scr.shape, F32)
        acc_scr[...] = jnp.zeros(acc_scr.shape, F32)

    qrows = qr_scr[...]
    scores = []
    m_old = m_scr[...]
    m_new = m_old
    for i in range(pages):
        table = mlast_ref[0] if i == pages - 1 else mask_ref[...]
        s = lax.dot_general(qrows, kp[i][...].astype(BF16), _NT, preferred_element_type=F32) + table
        m_new = jnp.maximum(m_new, jnp.max(s, axis=1, keepdims=True))
        scores.append(s)
    alpha = jnp.exp2(m_old - m_new)
    l = alpha * l_scr[...]
    acc = alpha * acc_scr[...]
    for i in range(pages):
        p = jnp.exp2(scores[i] - m_new)
        l = l + jnp.sum(p, axis=1, keepdims=True)
        acc = acc + jnp.dot(p.astype(BF16), vp[i][...].astype(BF16), preferred_element_type=F32)
    carry = (m_new, l, acc)
    m_scr[...], l_scr[...], acc_scr[...] = carry

    @pl.when(c == pl.num_programs(1) - 1)
    def _():
        lam = _lam(lq1, lk1, lq2, lk2, lam_init)
        pad = jnp.zeros((qrows.shape[0] - kn_ref.shape[0], hw), F32)
        kn = jnp.concatenate([kn_ref[...], pad], axis=0).astype(BF16)
        vn = jnp.concatenate([vn_ref[...], pad], axis=0).astype(BF16)
        s = lax.dot_general(qrows, kn, _NT, preferred_element_type=F32) + bn_ref[...]
        _, l, acc = _online_softmax_step(s, *carry, vn)
        o = acc / l
        for h in range(n_heads):
            o_ref[:, h * hw:(h + 1) * hw] = _diff_finish(o[h * 2 * t:(h + 1) * 2 * t], t, lam, gs_ref[...], lam_init)


def _attn_sample(page_table, proj, k_new, v_new, cache_k, cache_v, base, mask, mask_last, bias_new, lams, gs,
                 *, n_seq, t, n_heads, hd, pages, lam_init):
    hw = 2 * hd
    vw = n_heads * hw
    nrow = n_heads * 2 * t
    n_pages = page_table.shape[1]
    assert n_pages % pages == 0 and nrow == LANES
    n_chunks = n_pages // pages
    page_rows = cache_k.shape[1]
    kern = functools.partial(_attn_sample_kernel, pages=pages, n_heads=n_heads, hd=hd, t=t,
                             scale=hd ** -0.5 * LOG2E, lam_init=lam_init)

    def page_spec(i):
        return pl.BlockSpec((None, page_rows, hw), lambda b, c, pt: (base + pt[b, c * pages + i], 0, 0))

    vec = lambda a: pl.BlockSpec(a.shape, lambda b, c, pt: (0,) * a.ndim)
    grid_spec = pltpu.PrefetchScalarGridSpec(
        num_scalar_prefetch=1,
        grid=(n_seq, n_chunks),
        in_specs=[pl.BlockSpec((t, vw), lambda b, c, pt: (b, 0)),
                  pl.BlockSpec((None, t * n_heads, hw), lambda b, c, pt: (b, 0, 0)),
                  pl.BlockSpec((None, t * n_heads, hw), lambda b, c, pt: (b, 0, 0)),
                  *[page_spec(i) for i in range(pages)],
                  *[page_spec(i) for i in range(pages)],
                  vec(mask),
                  pl.BlockSpec((1,) + mask_last.shape[1:],
                               lambda b, c, pt: (jnp.where(c == n_chunks - 1, 1, 0), 0, 0)),
                  vec(bias_new), *[vec(a) for a in lams], vec(gs)],
        out_specs=pl.BlockSpec((t, vw), lambda b, c, pt: (b, 0)),
        scratch_shapes=[pltpu.VMEM((nrow, hw), BF16),
                        pltpu.VMEM((nrow, 1), F32),
                        pltpu.VMEM((nrow, 1), F32),
                        pltpu.VMEM((nrow, hw), F32)],
    )
    return pl.pallas_call(
        kern,
        grid_spec=grid_spec,
        out_shape=jax.ShapeDtypeStruct((n_seq * t, vw), F32),
        compiler_params=_cparams("parallel", "arbitrary"),
        name="attn_sample",
    )(page_table, proj, k_new, v_new, *([cache_k] * pages), *([cache_v] * pages),
      mask, mask_last, bias_new, *lams, gs)


def _post_kernel(x_ref, o_ref, r_ref, ga_ref, gr_ref, p_ref, bm_ref, wo_ref, gf_ref, wfi_ref, wfo_ref,
                 gp_ref, wpg_ref, wpp_ref, gfin_ref, y_ref, *, d_ff, fc, final_norm):
    bm = bm_ref[...]
    mix = (jax.nn.sigmoid(ga_ref[...] + bm[0:1]) * o_ref[...]
           + jax.nn.sigmoid(gr_ref[...] + bm[1:2]) * r_ref[...])
    h = x_ref[...] + jnp.dot(mix.astype(BF16), wo_ref[...], preferred_element_type=F32)
    hn = _rms(h, gf_ref[...]).astype(BF16)
    acc = jnp.zeros_like(h)
    for c in range(d_ff // fc):
        gate = jnp.dot(hn, wfi_ref[:, c * fc:(c + 1) * fc], preferred_element_type=F32)
        up = jnp.dot(hn, wfi_ref[:, d_ff + c * fc:d_ff + (c + 1) * fc], preferred_element_type=F32)
        act = (jax.nn.silu(gate) * up).astype(BF16)
        acc = acc + jnp.dot(act, wfo_ref[c * fc:(c + 1) * fc, :], preferred_element_type=F32)
    h = h + acc
    hp = _rms(h, gp_ref[...]).astype(BF16)
    gate = jax.nn.sigmoid(jnp.dot(hp, wpg_ref[...], preferred_element_type=F32))
    h = h + gate * jnp.dot(p_ref[...].astype(BF16), wpp_ref[...], preferred_element_type=F32)
    y_ref[...] = _rms(h, gfin_ref[...]) if final_norm else h


def _post(x, o, r, proj, col_ga, col_gr, p, bm, wo, gf, wfi, wfo, gp, wpg, wpp, gfin, *, tm, fc, final_norm):
    n, d = x.shape
    d_ff = wfo.shape[0]
    assert d_ff % fc == 0
    kern = functools.partial(_post_kernel, d_ff=d_ff, fc=fc, final_norm=final_norm)
    tok = lambda col: pl.BlockSpec((tm, d), lambda i: (i, col))
    return pl.pallas_call(
        kern,
        grid=(n // tm,),
        in_specs=[tok(0), tok(0), tok(0), tok(col_ga), tok(col_gr),
                  pl.BlockSpec((tm, p.shape[1]), lambda i: (i, 0)),
                  *[_const_spec(a.shape) for a in (bm, wo, gf, wfi, wfo, gp, wpg, wpp, gfin)]],
        out_specs=pl.BlockSpec((tm, d), lambda i: (i, 0)),
        out_shape=jax.ShapeDtypeStruct((n, d), F32),
        compiler_params=_cparams("parallel"),
        name="post",
    )(x, o, r, proj, proj, p, bm, wo, gf, wfi, wfo, gp, wpg, wpp, gfin)


def _t5_bucket(n, n_buckets):
    max_exact = n_buckets // 2
    nf = jnp.maximum(n, 1).astype(F32)
    large = max_exact + (jnp.log(nf / max_exact) / math.log(MAX_DISTANCE / max_exact)
                         * (n_buckets - max_exact)).astype(jnp.int32)
    large = jnp.minimum(large, n_buckets - 1)
    return jnp.where(n < max_exact, n, large)


def _far_distance(n_buckets):
    n = np.arange(1, 4 * MAX_DISTANCE, dtype=np.int64)
    max_exact = n_buckets // 2
    large = max_exact + (np.log(n.astype(np.float32) / max_exact) / math.log(MAX_DISTANCE / max_exact)
                         * (n_buckets - max_exact)).astype(np.int64)
    bucket = np.where(n < max_exact, n, np.minimum(large, n_buckets - 1))
    not_last = np.nonzero(bucket != n_buckets - 1)[0]
    return int(n[not_last[-1]]) + 2


def _shifted_bias(rel_bias, dist):
    nb, nh = rel_bias.shape
    rel = rel_bias.astype(F32) - rel_bias[nb - 1].astype(F32)
    bucket = _t5_bucket(jnp.maximum(dist, 0), nb)
    b = jnp.zeros((nh,) + dist.shape, F32)
    for k in range(nb - 1):
        b = jnp.where(bucket == k, rel[k].reshape((nh,) + (1,) * dist.ndim), b)
    return jnp.where(dist >= 0, b * LOG2E, NEG_INF)


def _gate_weights(w_gx, w_ga):
    nb, bw, _ = w_gx.shape
    per = LANES // bw

    def blockdiag(wb):
        wb = wb.reshape(nb // per, per, bw, bw)
        eye = jnp.eye(per, dtype=wb.dtype)
        return jnp.einsum('gpij,pq->gpiqj', wb, eye).reshape(nb // per, LANES, LANES)

    return jnp.concatenate([blockdiag(w_gx), blockdiag(w_ga)], axis=-1).astype(BF16)


def kernel(x_prompt, x_sample, p_prompt, p_sample, cache_k, cache_v, state_conv, state_rnn, page_table,
           rel_bias, g_mix, w_in, conv_w, conv_b, w_gx, b_gx, w_ga, b_ga, a_param,
           lam_q1, lam_k1, lam_q2, lam_k2, g_subln, b_merge, w_out,
           g_ffn, w_ffn_in, w_ffn_out, g_ple, w_ple_gate, w_ple_proj, g_final):
    bp, sp, d = x_prompt.shape
    bs, ts, _ = x_sample.shape
    depth, n_pool, page_size, n_heads, hw = cache_k.shape
    hd = hw // 2
    n_pages = page_table.shape[1]
    past = n_pages * page_size
    lru_w = conv_w.shape[2]
    kw = conv_w.shape[1]
    assert d == n_heads * hw == lru_w and w_in.shape[2] == 7 * d and hw == LANES
    assert ts == SUBLANES and kw - 1 <= min(ts, SUBLANES) and sp % 256 == 0

    t_attn = 512
    far = _far_distance(rel_bias.shape[0])
    assert t_attn + 1 >= far and page_size + 1 >= far

    r = jnp.arange(t_attn, dtype=jnp.int32)
    diag = r[:, None] - r[None, :]
    bias_diag = _shifted_bias(rel_bias, diag)
    bias_prev = _shifted_bias(rel_bias, diag + t_attn)
    pages = 8
    tq = jnp.arange(ts, dtype=jnp.int32)
    kk = jnp.arange(page_size, dtype=jnp.int32)
    same_head = jnp.eye(n_heads, dtype=bool)[:, None, None, None, :]

    def head_table(b):
        full = jnp.where(same_head, b[:, None, :, :, None], NEG_INF)
        full = jnp.broadcast_to(full, (n_heads, 2, ts, b.shape[2], n_heads))
        return full.reshape(n_heads * 2 * ts, b.shape[2] * n_heads)

    mask = head_table(jnp.zeros((n_heads, ts, page_size), F32))
    mask_last = jnp.stack([mask, head_table(_shifted_bias(rel_bias, page_size + tq[:, None] - kk[None, :]))])
    bias_new = head_table(_shifted_bias(rel_bias, tq[:, None] - tq[None, :]))
    bias_new = jnp.pad(bias_new, ((0, 0), (0, LANES - ts * n_heads)), constant_values=NEG_INF)

    ck = cache_k.reshape(depth * n_pool, page_size * n_heads, hw)
    cv = cache_v.reshape(depth * n_pool, page_size * n_heads, hw)

    hp = x_prompt.reshape(bp * sp, d)
    hs = x_sample.reshape(bs * ts, d)
    row = lambda a: a.reshape(1, -1).astype(F32)
    outs = {k: [] for k in ("kp", "vp", "cp", "rp", "ks", "vs", "cs", "rs")}
    for l in range(depth):
        lam_init = 0.8 - 0.6 * math.exp(-0.3 * l)
        w_in_b = w_in[l].astype(BF16)
        wg = _gate_weights(w_gx[l], w_ga[l])
        lams = [row(a[l]) for a in (lam_q1, lam_k1, lam_q2, lam_k2)]
        gs = row(g_subln[l])
        rglru_w = (conv_w[l], row(conv_b[l]), wg, row(b_gx[l]), row(b_ga[l]), row(a_param[l]))
        post_w = (b_merge[l], w_out[l].astype(BF16), row(g_ffn[l]), w_ffn_in[l].astype(BF16),
                  w_ffn_out[l].astype(BF16), row(g_ple[l]), w_ple_gate[l].astype(BF16),
                  w_ple_proj[l].astype(BF16), row(g_final))
        last = l == depth - 1

        proj = _inproj(hp, row(g_mix[l]), w_in_b, tm=1024, tn=d)
        r_p, hl_p = _rglru(proj, 3, 4, jnp.zeros((bp, SUBLANES, lru_w), F32), jnp.zeros((bp, 1, lru_w), F32),
                           *rglru_w, n_seq=bp, seq_len=sp, rows=256, reset_first=True)
        o_p = _attn_prompt(proj, bias_diag, bias_prev, lams, gs, n_seq=bp, seq_len=sp, n_heads=n_heads,
                           hd=hd, t=t_attn, hps=2, lam_init=lam_init)
        hp = _post(hp, o_p, r_p, proj, 5, 6, p_prompt[l].reshape(bp * sp, -1), *post_w,
                   tm=256, fc=256, final_norm=last)
        proj3 = proj.reshape(bp, sp, 7 * d)
        outs["kp"].append(proj3[:, :, d:2 * d].reshape(bp, sp, n_heads, hw))
        outs["vp"].append(proj3[:, :, 2 * d:3 * d].reshape(bp, sp, n_heads, hw))
        outs["cp"].append(proj3[:, sp - (kw - 1):, 3 * d:4 * d])
        outs["rp"].append(hl_p[:, SUBLANES - 1])

        proj = _inproj(hs, row(g_mix[l]), w_in_b, tm=bs * ts, tn=d)
        tail0 = jnp.pad(state_conv[l].astype(F32), ((0, 0), (SUBLANES - (kw - 1), 0), (0, 0)))
        r_s, hl_s = _rglru(proj, 3, 4, tail0, state_rnn[l].astype(F32)[:, None, :],
                           *rglru_w, n_seq=bs, seq_len=ts, rows=ts, reset_first=(past == 0))
        proj3 = proj.reshape(bs, ts, 7 * d)
        k_s = proj3[:, :, d:2 * d].reshape(bs, ts, n_heads, hw)
        v_s = proj3[:, :, 2 * d:3 * d].reshape(bs, ts, n_heads, hw)
        o_s = _attn_sample(page_table, proj, k_s.reshape(bs, ts * n_heads, hw), v_s.reshape(bs, ts * n_heads, hw),
                           ck, cv, l * n_pool, mask, mask_last, bias_new, lams, gs,
                           n_seq=bs, t=ts, n_heads=n_heads, hd=hd, pages=pages, lam_init=lam_init)
        hs = _post(hs, o_s, r_s, proj, 5, 6, p_sample[l].reshape(bs * ts, -1), *post_w,
                   tm=bs * ts, fc=256, final_norm=last)
        outs["ks"].append(k_s)
        outs["vs"].append(v_s)
        outs["cs"].append(proj3[:, ts - (kw - 1):, 3 * d:4 * d])
        outs["rs"].append(hl_s[:, SUBLANES - 1])

    st = lambda k: jnp.stack(outs[k])
    return (hp.reshape(bp, sp, d), hs.reshape(bs, ts, d),
            st("kp"), st("vp"), st("cp"), st("rp"), st("ks"), st("vs"), st("cs"), st("rs"))
```

```python
import functools
import math

import jax
import jax.numpy as jnp
import numpy as np
from jax import lax
from jax.experimental import pallas as pl
from jax.experimental.pallas import tpu as pltpu

F32 = jnp.float32
BF16 = jnp.bfloat16

EPS = 1e-6
NEG_INF = -1e30
MAX_DISTANCE = 128
LRU_C = 8.0
LOG2E = math.log2(math.e)
LANES = 128
SUBLANES = 8
VMEM_LIMIT = 56 * 1024 * 1024


def _cparams(*sem):
    return pltpu.CompilerParams(dimension_semantics=sem, vmem_limit_bytes=VMEM_LIMIT)


def _rms(x, g):
    return x * lax.rsqrt(jnp.mean(x * x, axis=-1, keepdims=True) + EPS) * g


def _const_spec(shape):
    nd = len(shape)
    return pl.BlockSpec(shape, lambda *_: (0,) * nd, pipeline_mode=pl.Buffered(1))


def _inproj_kernel(x_ref, g_ref, w_ref, o_ref, k_ref, v_ref, xn_ref, *, k_col, v_col, n_heads):
    j = pl.program_id(1)

    @pl.when(j == 0)
    def _():
        xn_ref[...] = _rms(x_ref[...], g_ref[...]).astype(BF16)

    res = jnp.dot(xn_ref[...], w_ref[...], preferred_element_type=F32)
    o_ref[...] = res

    def scatter_heads(dst_ref):
        tm, hw = res.shape[0], res.shape[1] // n_heads
        for h in range(n_heads):
            dst_ref[pl.ds(h, tm, stride=n_heads), :] = res[:, h * hw:(h + 1) * hw]

    pl.when(j == k_col)(lambda: scatter_heads(k_ref))
    pl.when(j == v_col)(lambda: scatter_heads(v_ref))


def _inproj(x, g, w_bf16, tm, tn, k_col, v_col, n_heads):
    n, d = x.shape
    nw = w_bf16.shape[1]
    hw = tn // n_heads
    kern = functools.partial(_inproj_kernel, k_col=k_col, v_col=v_col, n_heads=n_heads)
    return pl.pallas_call(
        kern,
        grid=(n // tm, nw // tn),
        in_specs=[pl.BlockSpec((tm, d), lambda i, j: (i, 0)),
                  pl.BlockSpec((1, d), lambda i, j: (0, 0)),
                  pl.BlockSpec((d, tn), lambda i, j: (0, j))],
        out_specs=[pl.BlockSpec((tm, tn), lambda i, j: (i, j)),
                   pl.BlockSpec((tm * n_heads, hw), lambda i, j: (i, 0)),
                   pl.BlockSpec((tm * n_heads, hw), lambda i, j: (i, 0))],
        out_shape=[jax.ShapeDtypeStruct((n, nw), F32),
                   jax.ShapeDtypeStruct((n * n_heads, hw), F32),
                   jax.ShapeDtypeStruct((n * n_heads, hw), F32)],
        scratch_shapes=[pltpu.VMEM((tm, d), BF16)],
        compiler_params=_cparams("parallel", "arbitrary"),
        name="inproj",
    )(x, g, w_bf16)


def _rglru_kernel(xr_ref, yr_ref, tail_ref, h0_ref, cw_ref, cb_ref, wg_ref, bgx_ref, bga_ref, ap_ref,
                  r_ref, hlast_ref, tail_scr, carry_scr, *, nseq, rows, reset_first):
    s = pl.program_id(1)
    w = xr_ref.shape[-1]

    @pl.when(s == 0)
    def _():
        tail_scr[...] = tail_ref[...]
        carry_scr[...] = h0_ref[...]

    x = xr_ref[...].reshape(nseq, rows, w)
    tail = tail_scr[...]
    cw = cw_ref[...]
    kw = cw.shape[0]
    row8 = lax.broadcasted_iota(jnp.int32, (1, SUBLANES, w), 1)
    xc = cb_ref[...] + x * cw[kw - 1:kw]
    for k in range(1, kw):
        xs = pltpu.roll(x, k, axis=1)
        head = jnp.where(row8 < k, pltpu.roll(tail, k, axis=1), xs[:, :SUBLANES, :])
        xs = jnp.concatenate([head, xs[:, SUBLANES:, :]], axis=1) if rows > SUBLANES else head
        xc = xc + xs * cw[kw - 1 - k:kw - k]
    tail_scr[...] = x[:, rows - SUBLANES:, :]

    xc2 = xc.reshape(nseq * rows, w)
    xcb = xc2.astype(BF16)
    gx_pre, ga_pre = [], []
    for g in range(w // LANES):
        pre = jnp.dot(xcb[:, g * LANES:(g + 1) * LANES], wg_ref[g], preferred_element_type=F32)
        gx_pre.append(pre[:, :LANES])
        ga_pre.append(pre[:, LANES:])
    gx = jax.nn.sigmoid(jnp.concatenate(gx_pre, axis=1) + bgx_ref[...])
    ga = jax.nn.sigmoid(jnp.concatenate(ga_pre, axis=1) + bga_ref[...])

    z = -ap_ref[...]
    softplus = jnp.maximum(z, 0.0) + jnp.log1p(jnp.exp(-jnp.abs(z)))
    log_a = -LRU_C * ga * softplus
    a = jnp.exp(log_a)
    mult = jnp.sqrt(-jnp.tanh(log_a) * (a * a + 1.0))
    if reset_first:
        t_in_seq = lax.broadcasted_iota(jnp.int32, (nseq * rows, w), 0)
        if nseq > 1:
            t_in_seq = t_in_seq % rows
        first = t_in_seq == jnp.where(s == 0, 0, -1)
        a = jnp.where(first, 0.0, a)
        mult = jnp.where(first, 1.0, mult)
    b = xc2 * gx * mult

    ngrp = nseq * rows // SUBLANES
    av = a.reshape(ngrp, SUBLANES, w)
    bv = b.reshape(ngrp, SUBLANES, w)
    d = 1
    while d < SUBLANES:
        keep = row8 >= d
        a_prev = jnp.where(keep, pltpu.roll(av, d, axis=1), 1.0)
        b_prev = jnp.where(keep, pltpu.roll(bv, d, axis=1), 0.0)
        bv = av * b_prev + bv
        av = av * a_prev
        d *= 2
    av = av.reshape(nseq, rows, w)
    bv = bv.reshape(nseq, rows, w)
    carry = carry_scr[...]
    hs = []
    for g in range(rows // SUBLANES):
        sl = slice(g * SUBLANES, (g + 1) * SUBLANES)
        hg = av[:, sl, :] * carry + bv[:, sl, :]
        carry = hg[:, SUBLANES - 1:, :]
        hs.append(hg)
    carry_scr[...] = carry
    hlast_ref[...] = hs[-1]
    h = jnp.concatenate(hs, axis=1) if len(hs) > 1 else hs[0]
    r_ref[...] = h.reshape(nseq * rows, w) * jax.nn.gelu(yr_ref[...])


def _rglru(proj, col_x, col_y, tail0, h0, cw, cb, wg, bgx, bga, ap, *, n_seq, seq_len, rows, reset_first):
    w = cw.shape[1]
    n = n_seq * seq_len
    if seq_len >= rows:
        assert seq_len % rows == 0
        nseq_t, steps, grid0 = 1, seq_len // rows, n_seq
    else:
        assert seq_len == rows == SUBLANES
        nseq_t, steps, grid0 = n_seq, 1, 1
    tm = nseq_t * rows
    kern = functools.partial(_rglru_kernel, nseq=nseq_t, rows=rows, reset_first=reset_first)
    small = lambda shape: pl.BlockSpec(shape, lambda b, s: (0,) * len(shape))
    return pl.pallas_call(
        kern,
        grid=(grid0, steps),
        in_specs=[pl.BlockSpec((tm, w), lambda b, s: (b * steps + s, col_x)),
                  pl.BlockSpec((tm, w), lambda b, s: (b * steps + s, col_y)),
                  pl.BlockSpec((nseq_t, SUBLANES, w), lambda b, s: (b, 0, 0)),
                  pl.BlockSpec((nseq_t, 1, w), lambda b, s: (b, 0, 0)),
                  small(cw.shape), small(cb.shape), small(wg.shape),
                  small(bgx.shape), small(bga.shape), small(ap.shape)],
        out_specs=[pl.BlockSpec((tm, w), lambda b, s: (b * steps + s, 0)),
                   pl.BlockSpec((nseq_t, SUBLANES, w), lambda b, s: (b, 0, 0))],
        out_shape=[jax.ShapeDtypeStruct((n, w), F32),
                   jax.ShapeDtypeStruct((n_seq, SUBLANES, w), F32)],
        scratch_shapes=[pltpu.VMEM((nseq_t, SUBLANES, w), F32),
                        pltpu.VMEM((nseq_t, 1, w), F32)],
        compiler_params=_cparams("parallel", "arbitrary"),
        name="rglru",
    )(proj, proj, tail0, h0, cw, cb, wg, bgx, bga, ap)


def _lam(lq1, lk1, lq2, lk2, lam_init):
    return (jnp.exp(jnp.sum(lq1[...] * lk1[...], axis=-1, keepdims=True))
            - jnp.exp(jnp.sum(lq2[...] * lk2[...], axis=-1, keepdims=True)) + lam_init)


def _two_map_queries(q, hd):
    lane = lax.broadcasted_iota(jnp.int32, q.shape, 1)
    return jnp.concatenate([jnp.where(lane < hd, q, 0.0), jnp.where(lane >= hd, q, 0.0)], axis=0).astype(BF16)


def _softmax_numerators(s, m):
    m_new = jnp.maximum(m, jnp.max(s, axis=1, keepdims=True))
    return m_new, jnp.exp2(m - m_new), jnp.exp2(s - m_new)


def _online_softmax_step(s, m, l, acc, v):
    m_new, alpha, p = _softmax_numerators(s, m)
    l = alpha * l + jnp.sum(p, axis=1, keepdims=True)
    acc = alpha * acc + jnp.dot(p.astype(BF16), v, preferred_element_type=F32)
    return m_new, l, acc


def _diff_finish(o, t, lam, g, lam_init):
    return _rms(o[:t] - lam * o[t:], g) * (1.0 - lam_init)


_NT = (((1,), (1,)), ((), ()))


def _attn_prompt_kernel(q_ref, k_ref, v_ref, bd_ref, bp_ref, lq1, lk1, lq2, lk2, gs_ref, o_ref,
                        kb_ref, vb_ref, *state, t, hd, hps, scale, lam_init):
    qi = pl.program_id(2)
    hw = 2 * hd

    @pl.when(qi == 0)
    def _():
        kb_ref[...] = k_ref[...].astype(BF16)
        for g in range(hps):
            vb_ref[:, 2 * g * hw:(2 * g + 1) * hw] = v_ref[:, g * hw:(g + 1) * hw].astype(BF16)
            vb_ref[:, (2 * g + 1) * hw:(2 * g + 2) * hw] = jnp.ones((vb_ref.shape[0], hw), BF16)

    q = q_ref[...] * scale
    lane = lax.broadcasted_iota(jnp.int32, (t, hw), 1)
    chains = []
    for g in range(hps):
        qg = q[:, g * hw:(g + 1) * hw]
        chains.append((g, jnp.where(lane < hd, qg, 0.0).astype(BF16)))
        chains.append((g, jnp.where(lane >= hd, qg, 0.0).astype(BF16)))

    m_scrs, acc_scrs = state[:len(chains)], state[len(chains):]
    for m_scr, acc_scr in zip(m_scrs, acc_scrs):
        m_scr[...] = jnp.full(m_scr.shape, NEG_INF, F32)
        acc_scr[...] = jnp.zeros(acc_scr.shape, F32)

    def chunk(j, bias_ref):
        start = pl.multiple_of(j * t, t)
        for (g, qz), m_scr, acc_scr in zip(chains, m_scrs, acc_scrs):
            s = lax.dot_general(qz, kb_ref[pl.ds(start, t), g * hw:(g + 1) * hw], _NT, preferred_element_type=F32)
            if bias_ref is not None:
                s = s + bias_ref[g]
            m_old = m_scr[...]
            m_new = jnp.maximum(m_old, jnp.max(s, axis=1, keepdims=True))
            alpha = jnp.exp2(m_old - m_new)
            p = jnp.exp2(s - jnp.concatenate([m_new] * (t // LANES), axis=1))
            m_scr[...] = m_new
            acc_scr[...] = jnp.concatenate([alpha] * (2 * hw // LANES), axis=1) * acc_scr[...] + jnp.dot(
                p.astype(BF16), vb_ref[pl.ds(start, t), 2 * g * hw:(2 * g + 2) * hw], preferred_element_type=F32)

    def far_chunk(j, carry):
        chunk(j, None)
        return carry

    lax.fori_loop(0, jnp.maximum(qi - 1, 0), far_chunk, 0)

    @pl.when(qi >= 1)
    def _():
        chunk(qi - 1, bp_ref)

    chunk(qi, bd_ref)
    lam = _lam(lq1, lk1, lq2, lk2, lam_init)
    for g in range(hps):
        o = jnp.concatenate([acc_scrs[2 * g + c][:, :hw] / acc_scrs[2 * g + c][:, hw:] for c in range(2)], axis=0)
        o_ref[:, g * hw:(g + 1) * hw] = _diff_finish(o, t, lam, gs_ref[...], lam_init)


def _attn_prompt(proj, bias_diag, bias_prev, lams, gs, *, n_seq, seq_len, n_heads, hd, t, hps, lam_init):
    n, vw = n_seq * seq_len, n_heads * 2 * hd
    nq = seq_len // t
    bw = hps * 2 * hd
    ng = n_heads // hps
    kern = functools.partial(_attn_prompt_kernel, t=t, hd=hd, hps=hps, scale=hd ** -0.5 * LOG2E, lam_init=lam_init)
    vec = lambda a: pl.BlockSpec(a.shape, lambda h, b, i: (0, 0))
    return pl.pallas_call(
        kern,
        grid=(ng, n_seq, nq),
        in_specs=[pl.BlockSpec((t, bw), lambda h, b, i: (b * nq + i, h)),
                  pl.BlockSpec((seq_len, bw), lambda h, b, i: (b, ng + h)),
                  pl.BlockSpec((seq_len, bw), lambda h, b, i: (b, 2 * ng + h)),
                  pl.BlockSpec((hps, t, t), lambda h, b, i: (h, 0, 0)),
                  pl.BlockSpec((hps, t, t), lambda h, b, i: (h, 0, 0)),
                  *[vec(a) for a in lams], vec(gs)],
        out_specs=pl.BlockSpec((t, bw), lambda h, b, i: (b * nq + i, h)),
        out_shape=jax.ShapeDtypeStruct((n, vw), F32),
        scratch_shapes=[pltpu.VMEM((seq_len, bw), BF16), pltpu.VMEM((seq_len, 2 * bw), BF16),
                        *[pltpu.VMEM((t, LANES), F32)] * (2 * hps), *[pltpu.VMEM((t, 4 * hd), F32)] * (2 * hps)],
        compiler_params=_cparams("parallel", "parallel", "arbitrary"),
        name="attn_prompt",
    )(proj, proj, proj, bias_diag, bias_prev, *lams, gs)


def _attn_sample_kernel(pt_ref, q_ref, kn_ref, vn_ref, *rest, pages, group, n_heads, hd, t, scale, lam_init):
    kp = rest[:pages]
    vp = rest[pages:2 * pages]
    mask_ref, mlast_ref, bn_ref, lq1, lk1, lq2, lk2, gs_ref, o_ref, qr_scr, m_scr, l_scr, acc_scr = rest[2 * pages:]
    del pt_ref
    c = pl.program_id(1)
    hw = 2 * hd

    @pl.when(c == 0)
    def _():
        q = q_ref[...] * scale
        qr_scr[...] = jnp.concatenate(
            [_two_map_queries(q[:, h * hw:(h + 1) * hw], hd) for h in range(n_heads)], axis=0)
        m_scr[...] = jnp.full(m_scr.shape, NEG_INF, F32)
        l_scr[...] = jnp.zeros(l_scr.shape, F32)
        acc_scr[...] = jnp.zeros(acc_scr.shape, F32)

    qrows = qr_scr[...]
    carry = (m_scr[...], l_scr[...], acc_scr[...])
    for i0 in range(0, pages, group):
        m_old, l, acc = carry
        scores = []
        m_new = m_old
        for i in range(i0, i0 + group):
            table = mlast_ref[0] if i == pages - 1 else mask_ref[...]
            s = lax.dot_general(qrows, kp[i][...].astype(BF16), _NT, preferred_element_type=F32) + table
            m_new = jnp.maximum(m_new, jnp.max(s, axis=1, keepdims=True))
            scores.append(s)
        alpha = jnp.exp2(m_old - m_new)
        l = alpha * l
        acc = alpha * acc
        for i, s in zip(range(i0, i0 + group), scores):
            p = jnp.exp2(s - m_new)
            l = l + jnp.sum(p, axis=1, keepdims=True)
            acc = acc + jnp.dot(p.astype(BF16), vp[i][...].astype(BF16), preferred_element_type=F32)
        carry = (m_new, l, acc)
    m_scr[...], l_scr[...], acc_scr[...] = carry

    @pl.when(c == pl.num_programs(1) - 1)
    def _():
        lam = _lam(lq1, lk1, lq2, lk2, lam_init)
        pad = jnp.zeros((qrows.shape[0] - kn_ref.shape[0], hw), F32)
        kn = jnp.concatenate([kn_ref[...], pad], axis=0).astype(BF16)
        vn = jnp.concatenate([vn_ref[...], pad], axis=0).astype(BF16)
        s = lax.dot_general(qrows, kn, _NT, preferred_element_type=F32) + bn_ref[...]
        _, l, acc = _online_softmax_step(s, *carry, vn)
        o = acc / l
        for h in range(n_heads):
            o_ref[:, h * hw:(h + 1) * hw] = _diff_finish(o[h * 2 * t:(h + 1) * 2 * t], t, lam, gs_ref[...], lam_init)


def _attn_sample(page_table, proj, k_new, v_new, cache_k, cache_v, base, mask, mask_last, bias_new, lams, gs,
                 *, n_seq, t, n_heads, hd, pages, group, lam_init):
    hw = 2 * hd
    vw = n_heads * hw
    nrow = n_heads * 2 * t
    n_pages = page_table.shape[1]
    assert n_pages % pages == 0 and nrow == LANES
    n_chunks = n_pages // pages
    page_rows = cache_k.shape[1]
    assert pages % group == 0
    kern = functools.partial(_attn_sample_kernel, pages=pages, group=group, n_heads=n_heads, hd=hd, t=t,
                             scale=hd ** -0.5 * LOG2E, lam_init=lam_init)

    def page_spec(i):
        return pl.BlockSpec((None, page_rows, hw), lambda b, c, pt: (base + pt[b, c * pages + i], 0, 0))

    vec = lambda a: pl.BlockSpec(a.shape, lambda b, c, pt: (0,) * a.ndim)
    grid_spec = pltpu.PrefetchScalarGridSpec(
        num_scalar_prefetch=1,
        grid=(n_seq, n_chunks),
        in_specs=[pl.BlockSpec((t, vw), lambda b, c, pt: (b, 0)),
                  pl.BlockSpec((None, t * n_heads, hw), lambda b, c, pt: (b, 0, 0)),
                  pl.BlockSpec((None, t * n_heads, hw), lambda b, c, pt: (b, 0, 0)),
                  *[page_spec(i) for i in range(pages)],
                  *[page_spec(i) for i in range(pages)],
                  vec(mask),
                  pl.BlockSpec((1,) + mask_last.shape[1:],
                               lambda b, c, pt: (jnp.where(c == n_chunks - 1, 1, 0), 0, 0)),
                  vec(bias_new), *[vec(a) for a in lams], vec(gs)],
        out_specs=pl.BlockSpec((t, vw), lambda b, c, pt: (b, 0)),
        scratch_shapes=[pltpu.VMEM((nrow, hw), BF16),
                        pltpu.VMEM((nrow, 1), F32),
                        pltpu.VMEM((nrow, 1), F32),
                        pltpu.VMEM((nrow, hw), F32)],
    )
    return pl.pallas_call(
        kern,
        grid_spec=grid_spec,
        out_shape=jax.ShapeDtypeStruct((n_seq * t, vw), F32),
        compiler_params=_cparams("parallel", "arbitrary"),
        name="attn_sample",
    )(page_table, proj, k_new, v_new, *([cache_k] * pages), *([cache_v] * pages),
      mask, mask_last, bias_new, *lams, gs)


def _post_kernel(x_ref, o_ref, r_ref, ga_ref, gr_ref, p_ref, bm_ref, wo_ref, gf_ref, wfi_ref, wfo_ref,
                 gp_ref, wpg_ref, wpp_ref, gfin_ref, y_ref, *, d_ff, fc, final_norm):
    bm = bm_ref[...]
    mix = (jax.nn.sigmoid(ga_ref[...] + bm[0:1]) * o_ref[...]
           + jax.nn.sigmoid(gr_ref[...] + bm[1:2]) * r_ref[...])
    h = x_ref[...] + jnp.dot(mix.astype(BF16), wo_ref[...], preferred_element_type=F32)
    hn = _rms(h, gf_ref[...]).astype(BF16)
    acc = jnp.zeros_like(h)
    for c in range(d_ff // fc):
        gate = jnp.dot(hn, wfi_ref[:, c * fc:(c + 1) * fc], preferred_element_type=F32)
        up = jnp.dot(hn, wfi_ref[:, d_ff + c * fc:d_ff + (c + 1) * fc], preferred_element_type=F32)
        act = (jax.nn.silu(gate) * up).astype(BF16)
        acc = acc + jnp.dot(act, wfo_ref[c * fc:(c + 1) * fc, :], preferred_element_type=F32)
    h = h + acc
    hp = _rms(h, gp_ref[...]).astype(BF16)
    gate = jax.nn.sigmoid(jnp.dot(hp, wpg_ref[...], preferred_element_type=F32))
    h = h + gate * jnp.dot(p_ref[...].astype(BF16), wpp_ref[...], preferred_element_type=F32)
    y_ref[...] = _rms(h, gfin_ref[...]) if final_norm else h


def _post(x, o, r, proj, col_ga, col_gr, p, bm, wo, gf, wfi, wfo, gp, wpg, wpp, gfin, *, tm, fc, final_norm):
    n, d = x.shape
    d_ff = wfo.shape[0]
    assert d_ff % fc == 0
    kern = functools.partial(_post_kernel, d_ff=d_ff, fc=fc, final_norm=final_norm)
    tok = lambda col: pl.BlockSpec((tm, d), lambda i: (i, col))
    return pl.pallas_call(
        kern,
        grid=(n // tm,),
        in_specs=[tok(0), tok(0), tok(0), tok(col_ga), tok(col_gr),
                  pl.BlockSpec((tm, p.shape[1]), lambda i: (i, 0)),
                  *[_const_spec(a.shape) for a in (bm, wo, gf, wfi, wfo, gp, wpg, wpp, gfin)]],
        out_specs=pl.BlockSpec((tm, d), lambda i: (i, 0)),
        out_shape=jax.ShapeDtypeStruct((n, d), F32),
        compiler_params=_cparams("parallel"),
        name="post",
    )(x, o, r, proj, proj, p, bm, wo, gf, wfi, wfo, gp, wpg, wpp, gfin)


def _t5_bucket(n, n_buckets):
    max_exact = n_buckets // 2
    nf = jnp.maximum(n, 1).astype(F32)
    large = max_exact + (jnp.log(nf / max_exact) / math.log(MAX_DISTANCE / max_exact)
                         * (n_buckets - max_exact)).astype(jnp.int32)
    large = jnp.minimum(large, n_buckets - 1)
    return jnp.where(n < max_exact, n, large)


def _far_distance(n_buckets):
    n = np.arange(1, 4 * MAX_DISTANCE, dtype=np.int64)
    max_exact = n_buckets // 2
    large = max_exact + (np.log(n.astype(np.float32) / max_exact) / math.log(MAX_DISTANCE / max_exact)
                         * (n_buckets - max_exact)).astype(np.int64)
    bucket = np.where(n < max_exact, n, np.minimum(large, n_buckets - 1))
    not_last = np.nonzero(bucket != n_buckets - 1)[0]
    return int(n[not_last[-1]]) + 2


def _shifted_bias(rel_bias, dist):
    nb, nh = rel_bias.shape
    rel = rel_bias.astype(F32) - rel_bias[nb - 1].astype(F32)
    bucket = _t5_bucket(jnp.maximum(dist, 0), nb)
    b = jnp.zeros((nh,) + dist.shape, F32)
    for k in range(nb - 1):
        b = jnp.where(bucket == k, rel[k].reshape((nh,) + (1,) * dist.ndim), b)
    return jnp.where(dist >= 0, b * LOG2E, NEG_INF)


def _gate_weights(w_gx, w_ga):
    nb, bw, _ = w_gx.shape
    per = LANES // bw

    def blockdiag(wb):
        wb = wb.reshape(nb // per, per, bw, bw)
        eye = jnp.eye(per, dtype=wb.dtype)
        return jnp.einsum('gpij,pq->gpiqj', wb, eye).reshape(nb // per, LANES, LANES)

    return jnp.concatenate([blockdiag(w_gx), blockdiag(w_ga)], axis=-1).astype(BF16)


def kernel(x_prompt, x_sample, p_prompt, p_sample, cache_k, cache_v, state_conv, state_rnn, page_table,
           rel_bias, g_mix, w_in, conv_w, conv_b, w_gx, b_gx, w_ga, b_ga, a_param,
           lam_q1, lam_k1, lam_q2, lam_k2, g_subln, b_merge, w_out,
           g_ffn, w_ffn_in, w_ffn_out, g_ple, w_ple_gate, w_ple_proj, g_final):
    bp, sp, d = x_prompt.shape
    bs, ts, _ = x_sample.shape
    depth, n_pool, page_size, n_heads, hw = cache_k.shape
    hd = hw // 2
    n_pages = page_table.shape[1]
    past = n_pages * page_size
    lru_w = conv_w.shape[2]
    kw = conv_w.shape[1]
    assert d == n_heads * hw == lru_w and w_in.shape[2] == 7 * d and hw == LANES
    assert ts == SUBLANES and kw - 1 <= min(ts, SUBLANES) and sp % 256 == 0

    t_attn = 512
    far = _far_distance(rel_bias.shape[0])
    assert t_attn + 1 >= far and page_size + 1 >= far

    r = jnp.arange(t_attn, dtype=jnp.int32)
    diag = r[:, None] - r[None, :]
    bias_diag = _shifted_bias(rel_bias, diag)
    bias_prev = _shifted_bias(rel_bias, diag + t_attn)
    pages = 16
    tq = jnp.arange(ts, dtype=jnp.int32)
    kk = jnp.arange(page_size, dtype=jnp.int32)
    same_head = jnp.eye(n_heads, dtype=bool)[:, None, None, None, :]

    def head_table(b):
        full = jnp.where(same_head, b[:, None, :, :, None], NEG_INF)
        full = jnp.broadcast_to(full, (n_heads, 2, ts, b.shape[2], n_heads))
        return full.reshape(n_heads * 2 * ts, b.shape[2] * n_heads)

    mask = head_table(jnp.zeros((n_heads, ts, page_size), F32))
    mask_last = jnp.stack([mask, head_table(_shifted_bias(rel_bias, page_size + tq[:, None] - kk[None, :]))])
    bias_new = head_table(_shifted_bias(rel_bias, tq[:, None] - tq[None, :]))
    bias_new = jnp.pad(bias_new, ((0, 0), (0, LANES - ts * n_heads)), constant_values=NEG_INF)

    ck = cache_k.reshape(depth * n_pool, page_size * n_heads, hw)
    cv = cache_v.reshape(depth * n_pool, page_size * n_heads, hw)

    hp = x_prompt.reshape(bp * sp, d)
    hs = x_sample.reshape(bs * ts, d)
    row = lambda a: a.reshape(1, -1).astype(F32)
    outs = {k: [] for k in ("kp", "vp", "cp", "rp", "ks", "vs", "cs", "rs")}
    for l in range(depth):
        lam_init = 0.8 - 0.6 * math.exp(-0.3 * l)
        w_in_b = w_in[l].astype(BF16)
        wg = _gate_weights(w_gx[l], w_ga[l])
        lams = [row(a[l]) for a in (lam_q1, lam_k1, lam_q2, lam_k2)]
        gs = row(g_subln[l])
        rglru_w = (conv_w[l], row(conv_b[l]), wg, row(b_gx[l]), row(b_ga[l]), row(a_param[l]))
        post_w = (b_merge[l], w_out[l].astype(BF16), row(g_ffn[l]), w_ffn_in[l].astype(BF16),
                  w_ffn_out[l].astype(BF16), row(g_ple[l]), w_ple_gate[l].astype(BF16),
                  w_ple_proj[l].astype(BF16), row(g_final))
        last = l == depth - 1

        proj, k_rows, v_rows = _inproj(hp, row(g_mix[l]), w_in_b, tm=1024, tn=d, k_col=1, v_col=2, n_heads=n_heads)
        r_p, hl_p = _rglru(proj, 3, 4, jnp.zeros((bp, SUBLANES, lru_w), F32), jnp.zeros((bp, 1, lru_w), F32),
                           *rglru_w, n_seq=bp, seq_len=sp, rows=256, reset_first=True)
        o_p = _attn_prompt(proj, bias_diag, bias_prev, lams, gs, n_seq=bp, seq_len=sp, n_heads=n_heads,
                           hd=hd, t=t_attn, hps=2, lam_init=lam_init)
        hp = _post(hp, o_p, r_p, proj, 5, 6, p_prompt[l].reshape(bp * sp, -1), *post_w,
                   tm=256, fc=256, final_norm=last)
        proj3 = proj.reshape(bp, sp, 7 * d)
        outs["kp"].append(k_rows.reshape(bp, sp, n_heads, hw))
        outs["vp"].append(v_rows.reshape(bp, sp, n_heads, hw))
        outs["cp"].append(proj3[:, sp - (kw - 1):, 3 * d:4 * d])
        outs["rp"].append(hl_p[:, SUBLANES - 1])

        proj, k_rows, v_rows = _inproj(hs, row(g_mix[l]), w_in_b, tm=bs * ts, tn=d, k_col=1, v_col=2,
                                       n_heads=n_heads)
        tail0 = jnp.pad(state_conv[l].astype(F32), ((0, 0), (SUBLANES - (kw - 1), 0), (0, 0)))
        r_s, hl_s = _rglru(proj, 3, 4, tail0, state_rnn[l].astype(F32)[:, None, :],
                           *rglru_w, n_seq=bs, seq_len=ts, rows=ts, reset_first=(past == 0))
        proj3 = proj.reshape(bs, ts, 7 * d)
        k_s = k_rows.reshape(bs, ts, n_heads, hw)
        v_s = v_rows.reshape(bs, ts, n_heads, hw)
        o_s = _attn_sample(page_table, proj, k_rows.reshape(bs, ts * n_heads, hw),
                           v_rows.reshape(bs, ts * n_heads, hw),
                           ck, cv, l * n_pool, mask, mask_last, bias_new, lams, gs,
                           n_seq=bs, t=ts, n_heads=n_heads, hd=hd, pages=pages, group=8, lam_init=lam_init)
        hs = _post(hs, o_s, r_s, proj, 5, 6, p_sample[l].reshape(bs * ts, -1), *post_w,
                   tm=bs * ts, fc=256, final_norm=last)
        outs["ks"].append(k_s)
        outs["vs"].append(v_s)
        outs["cs"].append(proj3[:, ts - (kw - 1):, 3 * d:4 * d])
        outs["rs"].append(hl_s[:, SUBLANES - 1])

    st = lambda k: jnp.stack(outs[k])
    return (hp.reshape(bp, sp, d), hs.reshape(bs, ts, d),
            st("kp"), st("vp"), st("cp"), st("rp"), st("ks"), st("vs"), st("cs"), st("rs"))
```

```python
import functools
import math

import jax
import jax.numpy as jnp
import numpy as np
from jax import lax
from jax.experimental import pallas as pl
from jax.experimental.pallas import tpu as pltpu

F32 = jnp.float32
BF16 = jnp.bfloat16

EPS = 1e-6
NEG_INF = -1e30
MAX_DISTANCE = 128
LRU_C = 8.0
LOG2E = math.log2(math.e)
LANES = 128
SUBLANES = 8
VMEM_LIMIT = 56 * 1024 * 1024


def _cparams(*sem):
    return pltpu.CompilerParams(dimension_semantics=sem, vmem_limit_bytes=VMEM_LIMIT)


def _rms(x, g):
    return x * lax.rsqrt(jnp.mean(x * x, axis=-1, keepdims=True) + EPS) * g


def _const_spec(shape):
    nd = len(shape)
    return pl.BlockSpec(shape, lambda *_: (0,) * nd, pipeline_mode=pl.Buffered(1))


def _inproj_kernel(x_ref, g_ref, w_ref, o_ref, k_ref, v_ref, *, tn, k_col, v_col, n_heads):
    xn = _rms(x_ref[...], g_ref[...]).astype(BF16)
    tm, hw = xn.shape[0], tn // n_heads
    for j in range(w_ref.shape[1] // tn):
        res = jnp.dot(xn, w_ref[:, j * tn:(j + 1) * tn], preferred_element_type=F32)
        o_ref[:, j * tn:(j + 1) * tn] = res
        for col, dst_ref in ((k_col, k_ref), (v_col, v_ref)):
            if j == col:
                for h in range(n_heads):
                    dst_ref[pl.ds(h, tm, stride=n_heads), :] = res[:, h * hw:(h + 1) * hw]


def _inproj(x, g, w_bf16, tm, tn, k_col, v_col, n_heads):
    n, d = x.shape
    nw = w_bf16.shape[1]
    hw = tn // n_heads
    kern = functools.partial(_inproj_kernel, tn=tn, k_col=k_col, v_col=v_col, n_heads=n_heads)
    return pl.pallas_call(
        kern,
        grid=(n // tm,),
        in_specs=[pl.BlockSpec((tm, d), lambda i: (i, 0)),
                  _const_spec((1, d)), _const_spec((d, nw))],
        out_specs=[pl.BlockSpec((tm, nw), lambda i: (i, 0)),
                   pl.BlockSpec((tm * n_heads, hw), lambda i: (i, 0)),
                   pl.BlockSpec((tm * n_heads, hw), lambda i: (i, 0))],
        out_shape=[jax.ShapeDtypeStruct((n, nw), F32),
                   jax.ShapeDtypeStruct((n * n_heads, hw), F32),
                   jax.ShapeDtypeStruct((n * n_heads, hw), F32)],
        compiler_params=_cparams("parallel"),
        name="inproj",
    )(x, g, w_bf16)


def _rglru_kernel(xr_ref, yr_ref, tail_ref, h0_ref, cw_ref, cb_ref, wg_ref, bgx_ref, bga_ref, ap_ref,
                  r_ref, hlast_ref, tail_scr, carry_scr, *, nseq, rows, reset_first):
    s = pl.program_id(1)
    w = xr_ref.shape[-1]

    @pl.when(s == 0)
    def _():
        tail_scr[...] = tail_ref[...]
        carry_scr[...] = h0_ref[...]

    x = xr_ref[...].reshape(nseq, rows, w)
    tail = tail_scr[...]
    cw = cw_ref[...]
    kw = cw.shape[0]
    row8 = lax.broadcasted_iota(jnp.int32, (1, SUBLANES, w), 1)
    xc = cb_ref[...] + x * cw[kw - 1:kw]
    for k in range(1, kw):
        xs = pltpu.roll(x, k, axis=1)
        head = jnp.where(row8 < k, pltpu.roll(tail, k, axis=1), xs[:, :SUBLANES, :])
        xs = jnp.concatenate([head, xs[:, SUBLANES:, :]], axis=1) if rows > SUBLANES else head
        xc = xc + xs * cw[kw - 1 - k:kw - k]
    tail_scr[...] = x[:, rows - SUBLANES:, :]

    xc2 = xc.reshape(nseq * rows, w)
    xcb = xc2.astype(BF16)
    gx_pre, ga_pre = [], []
    for g in range(w // LANES):
        pre = jnp.dot(xcb[:, g * LANES:(g + 1) * LANES], wg_ref[g], preferred_element_type=F32)
        gx_pre.append(pre[:, :LANES])
        ga_pre.append(pre[:, LANES:])
    gx = jax.nn.sigmoid(jnp.concatenate(gx_pre, axis=1) + bgx_ref[...])
    ga = jax.nn.sigmoid(jnp.concatenate(ga_pre, axis=1) + bga_ref[...])

    z = -ap_ref[...]
    softplus = jnp.maximum(z, 0.0) + jnp.log1p(jnp.exp(-jnp.abs(z)))
    log_a = -LRU_C * ga * softplus
    a = jnp.exp(log_a)
    mult = jnp.sqrt(-jnp.tanh(log_a) * (a * a + 1.0))
    if reset_first:
        t_in_seq = lax.broadcasted_iota(jnp.int32, (nseq * rows, w), 0)
        if nseq > 1:
            t_in_seq = t_in_seq % rows
        first = t_in_seq == jnp.where(s == 0, 0, -1)
        a = jnp.where(first, 0.0, a)
        mult = jnp.where(first, 1.0, mult)
    b = xc2 * gx * mult

    ngrp = nseq * rows // SUBLANES
    av = a.reshape(ngrp, SUBLANES, w)
    bv = b.reshape(ngrp, SUBLANES, w)
    d = 1
    while d < SUBLANES:
        keep = row8 >= d
        a_prev = jnp.where(keep, pltpu.roll(av, d, axis=1), 1.0)
        b_prev = jnp.where(keep, pltpu.roll(bv, d, axis=1), 0.0)
        bv = av * b_prev + bv
        av = av * a_prev
        d *= 2
    av = av.reshape(nseq, rows, w)
    bv = bv.reshape(nseq, rows, w)
    carry = carry_scr[...]
    hs = []
    for g in range(rows // SUBLANES):
        sl = slice(g * SUBLANES, (g + 1) * SUBLANES)
        hg = av[:, sl, :] * carry + bv[:, sl, :]
        carry = hg[:, SUBLANES - 1:, :]
        hs.append(hg)
    carry_scr[...] = carry
    hlast_ref[...] = hs[-1]
    h = jnp.concatenate(hs, axis=1) if len(hs) > 1 else hs[0]
    r_ref[...] = h.reshape(nseq * rows, w) * jax.nn.gelu(yr_ref[...])


def _rglru(proj, col_x, col_y, tail0, h0, cw, cb, wg, bgx, bga, ap, *, n_seq, seq_len, rows, reset_first):
    w = cw.shape[1]
    n = n_seq * seq_len
    if seq_len >= rows:
        assert seq_len % rows == 0
        nseq_t, steps, grid0 = 1, seq_len // rows, n_seq
    else:
        assert seq_len == rows == SUBLANES
        nseq_t, steps, grid0 = n_seq, 1, 1
    tm = nseq_t * rows
    kern = functools.partial(_rglru_kernel, nseq=nseq_t, rows=rows, reset_first=reset_first)
    small = lambda shape: pl.BlockSpec(shape, lambda b, s: (0,) * len(shape))
    return pl.pallas_call(
        kern,
        grid=(grid0, steps),
        in_specs=[pl.BlockSpec((tm, w), lambda b, s: (b * steps + s, col_x)),
                  pl.BlockSpec((tm, w), lambda b, s: (b * steps + s, col_y)),
                  pl.BlockSpec((nseq_t, SUBLANES, w), lambda b, s: (b, 0, 0)),
                  pl.BlockSpec((nseq_t, 1, w), lambda b, s: (b, 0, 0)),
                  small(cw.shape), small(cb.shape), small(wg.shape),
                  small(bgx.shape), small(bga.shape), small(ap.shape)],
        out_specs=[pl.BlockSpec((tm, w), lambda b, s: (b * steps + s, 0)),
                   pl.BlockSpec((nseq_t, SUBLANES, w), lambda b, s: (b, 0, 0))],
        out_shape=[jax.ShapeDtypeStruct((n, w), F32),
                   jax.ShapeDtypeStruct((n_seq, SUBLANES, w), F32)],
        scratch_shapes=[pltpu.VMEM((nseq_t, SUBLANES, w), F32),
                        pltpu.VMEM((nseq_t, 1, w), F32)],
        compiler_params=_cparams("parallel", "arbitrary"),
        name="rglru",
    )(proj, proj, tail0, h0, cw, cb, wg, bgx, bga, ap)


def _lam(lq1, lk1, lq2, lk2, lam_init):
    return (jnp.exp(jnp.sum(lq1[...] * lk1[...], axis=-1, keepdims=True))
            - jnp.exp(jnp.sum(lq2[...] * lk2[...], axis=-1, keepdims=True)) + lam_init)


def _two_map_queries(q, hd):
    lane = lax.broadcasted_iota(jnp.int32, q.shape, 1)
    return jnp.concatenate([jnp.where(lane < hd, q, 0.0), jnp.where(lane >= hd, q, 0.0)], axis=0).astype(BF16)


def _softmax_numerators(s, m):
    m_new = jnp.maximum(m, jnp.max(s, axis=1, keepdims=True))
    return m_new, jnp.exp2(m - m_new), jnp.exp2(s - m_new)


def _online_softmax_step(s, m, l, acc, v):
    m_new, alpha, p = _softmax_numerators(s, m)
    l = alpha * l + jnp.sum(p, axis=1, keepdims=True)
    acc = alpha * acc + jnp.dot(p.astype(BF16), v, preferred_element_type=F32)
    return m_new, l, acc


def _diff_finish(o, t, lam, g, lam_init):
    return _rms(o[:t] - lam * o[t:], g) * (1.0 - lam_init)


_NT = (((1,), (1,)), ((), ()))
_NOT_VISIBLE = object()


def _attn_prompt_kernel(q_ref, k_ref, v_ref, bd_ref, bp_ref, lq1, lk1, lq2, lk2, gs_ref, o_ref,
                        kb_ref, vb_ref, *state, t, hd, hps, nt, scale, lam_init):
    first = pl.program_id(2) * nt
    hw = 2 * hd

    @pl.when(first == 0)
    def _():
        kb_ref[...] = k_ref[...].astype(BF16)
        for g in range(hps):
            vb_ref[:, 2 * g * hw:(2 * g + 1) * hw] = v_ref[:, g * hw:(g + 1) * hw].astype(BF16)
            vb_ref[:, (2 * g + 1) * hw:(2 * g + 2) * hw] = jnp.ones((vb_ref.shape[0], hw), BF16)

    q = q_ref[...] * scale
    lane = lax.broadcasted_iota(jnp.int32, (t, hw), 1)
    chains = []
    for u in range(nt):
        for g in range(hps):
            qg = q[u * t:(u + 1) * t, g * hw:(g + 1) * hw]
            chains.append((u, g, jnp.where(lane < hd, qg, 0.0).astype(BF16)))
            chains.append((u, g, jnp.where(lane >= hd, qg, 0.0).astype(BF16)))

    m_scrs, acc_scrs = state[:len(chains)], state[len(chains):]
    for m_scr, acc_scr in zip(m_scrs, acc_scrs):
        m_scr[...] = jnp.full(m_scr.shape, NEG_INF, F32)
        acc_scr[...] = jnp.zeros(acc_scr.shape, F32)

    def chunk(j, bias_of_tile):
        start = pl.multiple_of(j * t, t)
        for (u, g, qz), m_scr, acc_scr in zip(chains, m_scrs, acc_scrs):
            bias_ref = bias_of_tile[u]
            if bias_ref is _NOT_VISIBLE:
                continue
            s = lax.dot_general(qz, kb_ref[pl.ds(start, t), g * hw:(g + 1) * hw], _NT, preferred_element_type=F32)
            if bias_ref is not None:
                s = s + bias_ref[g]
            m_old = m_scr[...]
            m_new = jnp.maximum(m_old, jnp.max(s, axis=1, keepdims=True))
            alpha = jnp.exp2(m_old - m_new)
            p = jnp.exp2(s - jnp.concatenate([m_new] * (t // LANES), axis=1))
            m_scr[...] = m_new
            acc_scr[...] = jnp.concatenate([alpha] * (2 * hw // LANES), axis=1) * acc_scr[...] + jnp.dot(
                p.astype(BF16), vb_ref[pl.ds(start, t), 2 * g * hw:(2 * g + 2) * hw], preferred_element_type=F32)

    def far_chunk(j, carry):
        chunk(j, [None] * nt)
        return carry

    lax.fori_loop(0, jnp.maximum(first - 1, 0), far_chunk, 0)
    for e in range(nt + 1):
        rel = [e - 1 - u for u in range(nt)]
        biases = [None if r <= -2 else bp_ref if r == -1 else bd_ref if r == 0 else _NOT_VISIBLE for r in rel]
        if e == 0:
            pl.when(first >= 1)(functools.partial(chunk, first - 1, biases))
        else:
            chunk(first + e - 1, biases)

    lam = _lam(lq1, lk1, lq2, lk2, lam_init)
    for u in range(nt):
        for g in range(hps):
            accs = [acc_scrs[(u * hps + g) * 2 + c] for c in range(2)]
            o = jnp.concatenate([a[:, :hw] / a[:, hw:] for a in accs], axis=0)
            o_ref[u * t:(u + 1) * t, g * hw:(g + 1) * hw] = _diff_finish(o, t, lam, gs_ref[...], lam_init)


def _attn_prompt(proj, bias_diag, bias_prev, lams, gs, *, n_seq, seq_len, n_heads, hd, t, hps, nt, lam_init):
    n, vw = n_seq * seq_len, n_heads * 2 * hd
    nq = seq_len // (t * nt)
    bw = hps * 2 * hd
    ng = n_heads // hps
    kern = functools.partial(_attn_prompt_kernel, t=t, hd=hd, hps=hps, nt=nt, scale=hd ** -0.5 * LOG2E,
                             lam_init=lam_init)
    vec = lambda a: pl.BlockSpec(a.shape, lambda h, b, i: (0, 0))
    return pl.pallas_call(
        kern,
        grid=(ng, n_seq, nq),
        in_specs=[pl.BlockSpec((nt * t, bw), lambda h, b, i: (b * nq + i, h)),
                  pl.BlockSpec((seq_len, bw), lambda h, b, i: (b, ng + h)),
                  pl.BlockSpec((seq_len, bw), lambda h, b, i: (b, 2 * ng + h)),
                  pl.BlockSpec((hps, t, t), lambda h, b, i: (h, 0, 0)),
                  pl.BlockSpec((hps, t, t), lambda h, b, i: (h, 0, 0)),
                  *[vec(a) for a in lams], vec(gs)],
        out_specs=pl.BlockSpec((nt * t, bw), lambda h, b, i: (b * nq + i, h)),
        out_shape=jax.ShapeDtypeStruct((n, vw), F32),
        scratch_shapes=[pltpu.VMEM((seq_len, bw), BF16), pltpu.VMEM((seq_len, 2 * bw), BF16),
                        *[pltpu.VMEM((t, LANES), F32)] * (2 * hps * nt),
                        *[pltpu.VMEM((t, 4 * hd), F32)] * (2 * hps * nt)],
        compiler_params=_cparams("parallel", "parallel", "arbitrary"),
        name="attn_prompt",
    )(proj, proj, proj, bias_diag, bias_prev, *lams, gs)


def _attn_sample_kernel(pt_ref, q_ref, kn_ref, vn_ref, *rest, pages, group, n_heads, hd, t, scale, lam_init):
    kp = rest[:pages]
    vp = rest[pages:2 * pages]
    mask_ref, mlast_ref, bn_ref, lq1, lk1, lq2, lk2, gs_ref, o_ref, qr_scr, m_scr, l_scr, acc_scr = rest[2 * pages:]
    del pt_ref
    c = pl.program_id(1)
    hw = 2 * hd

    @pl.when(c == 0)
    def _():
        q = q_ref[...] * scale
        qr_scr[...] = jnp.concatenate(
            [_two_map_queries(q[:, h * hw:(h + 1) * hw], hd) for h in range(n_heads)], axis=0)
        m_scr[...] = jnp.full(m_scr.shape, NEG_INF, F32)
        l_scr[...] = jnp.zeros(l_scr.shape, F32)
        acc_scr[...] = jnp.zeros(acc_scr.shape, F32)

    qrows = qr_scr[...]
    carry = (m_scr[...], l_scr[...], acc_scr[...])
    for i0 in range(0, pages, group):
        m_old, l, acc = carry
        scores = []
        m_new = m_old
        for i in range(i0, i0 + group):
            table = mlast_ref[0] if i == pages - 1 else mask_ref[...]
            s = lax.dot_general(qrows, kp[i][...].astype(BF16), _NT, preferred_element_type=F32) + table
            m_new = jnp.maximum(m_new, jnp.max(s, axis=1, keepdims=True))
            scores.append(s)
        alpha = jnp.exp2(m_old - m_new)
        l = alpha * l
        acc = alpha * acc
        for i, s in zip(range(i0, i0 + group), scores):
            p = jnp.exp2(s - m_new)
            l = l + jnp.sum(p, axis=1, keepdims=True)
            acc = acc + jnp.dot(p.astype(BF16), vp[i][...].astype(BF16), preferred_element_type=F32)
        carry = (m_new, l, acc)
    m_scr[...], l_scr[...], acc_scr[...] = carry

    @pl.when(c == pl.num_programs(1) - 1)
    def _():
        lam = _lam(lq1, lk1, lq2, lk2, lam_init)
        pad = jnp.zeros((qrows.shape[0] - kn_ref.shape[0], hw), F32)
        kn = jnp.concatenate([kn_ref[...], pad], axis=0).astype(BF16)
        vn = jnp.concatenate([vn_ref[...], pad], axis=0).astype(BF16)
        s = lax.dot_general(qrows, kn, _NT, preferred_element_type=F32) + bn_ref[...]
        _, l, acc = _online_softmax_step(s, *carry, vn)
        o = acc / l
        for h in range(n_heads):
            o_ref[:, h * hw:(h + 1) * hw] = _diff_finish(o[h * 2 * t:(h + 1) * 2 * t], t, lam, gs_ref[...], lam_init)


def _attn_sample(page_table, proj, k_new, v_new, cache_k, cache_v, base, mask, mask_last, bias_new, lams, gs,
                 *, n_seq, t, n_heads, hd, pages, group, lam_init):
    hw = 2 * hd
    vw = n_heads * hw
    nrow = n_heads * 2 * t
    n_pages = page_table.shape[1]
    assert n_pages % pages == 0 and nrow == LANES
    n_chunks = n_pages // pages
    page_rows = cache_k.shape[1]
    assert pages % group == 0
    kern = functools.partial(_attn_sample_kernel, pages=pages, group=group, n_heads=n_heads, hd=hd, t=t,
                             scale=hd ** -0.5 * LOG2E, lam_init=lam_init)

    def page_spec(i):
        return pl.BlockSpec((None, page_rows, hw), lambda b, c, pt: (base + pt[b, c * pages + i], 0, 0))

    vec = lambda a: pl.BlockSpec(a.shape, lambda b, c, pt: (0,) * a.ndim)
    grid_spec = pltpu.PrefetchScalarGridSpec(
        num_scalar_prefetch=1,
        grid=(n_seq, n_chunks),
        in_specs=[pl.BlockSpec((t, vw), lambda b, c, pt: (b, 0)),
                  pl.BlockSpec((None, t * n_heads, hw), lambda b, c, pt: (b, 0, 0)),
                  pl.BlockSpec((None, t * n_heads, hw), lambda b, c, pt: (b, 0, 0)),
                  *[page_spec(i) for i in range(pages)],
                  *[page_spec(i) for i in range(pages)],
                  vec(mask),
                  pl.BlockSpec((1,) + mask_last.shape[1:],
                               lambda b, c, pt: (jnp.where(c == n_chunks - 1, 1, 0), 0, 0)),
                  vec(bias_new), *[vec(a) for a in lams], vec(gs)],
        out_specs=pl.BlockSpec((t, vw), lambda b, c, pt: (b, 0)),
        scratch_shapes=[pltpu.VMEM((nrow, hw), BF16),
                        pltpu.VMEM((nrow, 1), F32),
                        pltpu.VMEM((nrow, 1), F32),
                        pltpu.VMEM((nrow, hw), F32)],
    )
    return pl.pallas_call(
        kern,
        grid_spec=grid_spec,
        out_shape=jax.ShapeDtypeStruct((n_seq * t, vw), F32),
        compiler_params=_cparams("parallel", "arbitrary"),
        name="attn_sample",
    )(page_table, proj, k_new, v_new, *([cache_k] * pages), *([cache_v] * pages),
      mask, mask_last, bias_new, *lams, gs)


def _post_kernel(x_ref, o_ref, r_ref, ga_ref, gr_ref, p_ref, bm_ref, wo_ref, gf_ref, wfi_ref, wfo_ref,
                 gp_ref, wpg_ref, wpp_ref, gfin_ref, y_ref, *, d_ff, fc, final_norm):
    bm = bm_ref[...]
    mix = (jax.nn.sigmoid(ga_ref[...] + bm[0:1]) * o_ref[...]
           + jax.nn.sigmoid(gr_ref[...] + bm[1:2]) * r_ref[...])
    h = x_ref[...] + jnp.dot(mix.astype(BF16), wo_ref[...], preferred_element_type=F32)
    hn = _rms(h, gf_ref[...]).astype(BF16)
    acc = jnp.zeros_like(h)
    for c in range(d_ff // fc):
        gate = jnp.dot(hn, wfi_ref[:, c * fc:(c + 1) * fc], preferred_element_type=F32)
        up = jnp.dot(hn, wfi_ref[:, d_ff + c * fc:d_ff + (c + 1) * fc], preferred_element_type=F32)
        act = (jax.nn.silu(gate) * up).astype(BF16)
        acc = acc + jnp.dot(act, wfo_ref[c * fc:(c + 1) * fc, :], preferred_element_type=F32)
    h = h + acc
    hp = _rms(h, gp_ref[...]).astype(BF16)
    gate = jax.nn.sigmoid(jnp.dot(hp, wpg_ref[...], preferred_element_type=F32))
    h = h + gate * jnp.dot(p_ref[...].astype(BF16), wpp_ref[...], preferred_element_type=F32)
    y_ref[...] = _rms(h, gfin_ref[...]) if final_norm else h


def _post(x, o, r, proj, col_ga, col_gr, p, bm, wo, gf, wfi, wfo, gp, wpg, wpp, gfin, *, tm, fc, final_norm):
    n, d = x.shape
    d_ff = wfo.shape[0]
    assert d_ff % fc == 0
    kern = functools.partial(_post_kernel, d_ff=d_ff, fc=fc, final_norm=final_norm)
    tok = lambda col: pl.BlockSpec((tm, d), lambda i: (i, col))
    return pl.pallas_call(
        kern,
        grid=(n // tm,),
        in_specs=[tok(0), tok(0), tok(0), tok(col_ga), tok(col_gr),
                  pl.BlockSpec((tm, p.shape[1]), lambda i: (i, 0)),
                  *[_const_spec(a.shape) for a in (bm, wo, gf, wfi, wfo, gp, wpg, wpp, gfin)]],
        out_specs=pl.BlockSpec((tm, d), lambda i: (i, 0)),
        out_shape=jax.ShapeDtypeStruct((n, d), F32),
        compiler_params=_cparams("parallel"),
        name="post",
    )(x, o, r, proj, proj, p, bm, wo, gf, wfi, wfo, gp, wpg, wpp, gfin)


def _t5_bucket(n, n_buckets):
    max_exact = n_buckets // 2
    nf = jnp.maximum(n, 1).astype(F32)
    large = max_exact + (jnp.log(nf / max_exact) / math.log(MAX_DISTANCE / max_exact)
                         * (n_buckets - max_exact)).astype(jnp.int32)
    large = jnp.minimum(large, n_buckets - 1)
    return jnp.where(n < max_exact, n, large)


def _far_distance(n_buckets):
    n = np.arange(1, 4 * MAX_DISTANCE, dtype=np.int64)
    max_exact = n_buckets // 2
    large = max_exact + (np.log(n.astype(np.float32) / max_exact) / math.log(MAX_DISTANCE / max_exact)
                         * (n_buckets - max_exact)).astype(np.int64)
    bucket = np.where(n < max_exact, n, np.minimum(large, n_buckets - 1))
    not_last = np.nonzero(bucket != n_buckets - 1)[0]
    return int(n[not_last[-1]]) + 2


def _shifted_bias(rel_bias, dist):
    nb, nh = rel_bias.shape
    rel = rel_bias.astype(F32) - rel_bias[nb - 1].astype(F32)
    bucket = _t5_bucket(jnp.maximum(dist, 0), nb)
    b = jnp.zeros((nh,) + dist.shape, F32)
    for k in range(nb - 1):
        b = jnp.where(bucket == k, rel[k].reshape((nh,) + (1,) * dist.ndim), b)
    return jnp.where(dist >= 0, b * LOG2E, NEG_INF)


def _gate_weights(w_gx, w_ga):
    nb, bw, _ = w_gx.shape
    per = LANES // bw

    def blockdiag(wb):
        wb = wb.reshape(nb // per, per, bw, bw)
        eye = jnp.eye(per, dtype=wb.dtype)
        return jnp.einsum('gpij,pq->gpiqj', wb, eye).reshape(nb // per, LANES, LANES)

    return jnp.concatenate([blockdiag(w_gx), blockdiag(w_ga)], axis=-1).astype(BF16)


def kernel(x_prompt, x_sample, p_prompt, p_sample, cache_k, cache_v, state_conv, state_rnn, page_table,
           rel_bias, g_mix, w_in, conv_w, conv_b, w_gx, b_gx, w_ga, b_ga, a_param,
           lam_q1, lam_k1, lam_q2, lam_k2, g_subln, b_merge, w_out,
           g_ffn, w_ffn_in, w_ffn_out, g_ple, w_ple_gate, w_ple_proj, g_final):
    bp, sp, d = x_prompt.shape
    bs, ts, _ = x_sample.shape
    depth, n_pool, page_size, n_heads, hw = cache_k.shape
    hd = hw // 2
    n_pages = page_table.shape[1]
    past = n_pages * page_size
    lru_w = conv_w.shape[2]
    kw = conv_w.shape[1]
    assert d == n_heads * hw == lru_w and w_in.shape[2] == 7 * d and hw == LANES
    assert ts == SUBLANES and kw - 1 <= min(ts, SUBLANES) and sp % 256 == 0

    t_attn = 512
    far = _far_distance(rel_bias.shape[0])
    assert t_attn + 1 >= far and page_size + 1 >= far

    r = jnp.arange(t_attn, dtype=jnp.int32)
    diag = r[:, None] - r[None, :]
    bias_diag = _shifted_bias(rel_bias, diag)
    bias_prev = _shifted_bias(rel_bias, diag + t_attn)
    pages = 16
    tq = jnp.arange(ts, dtype=jnp.int32)
    kk = jnp.arange(page_size, dtype=jnp.int32)
    same_head = jnp.eye(n_heads, dtype=bool)[:, None, None, None, :]

    def head_table(b):
        full = jnp.where(same_head, b[:, None, :, :, None], NEG_INF)
        full = jnp.broadcast_to(full, (n_heads, 2, ts, b.shape[2], n_heads))
        return full.reshape(n_heads * 2 * ts, b.shape[2] * n_heads)

    mask = head_table(jnp.zeros((n_heads, ts, page_size), F32))
    mask_last = jnp.stack([mask, head_table(_shifted_bias(rel_bias, page_size + tq[:, None] - kk[None, :]))])
    bias_new = head_table(_shifted_bias(rel_bias, tq[:, None] - tq[None, :]))
    bias_new = jnp.pad(bias_new, ((0, 0), (0, LANES - ts * n_heads)), constant_values=NEG_INF)

    ck = cache_k.reshape(depth * n_pool, page_size * n_heads, hw)
    cv = cache_v.reshape(depth * n_pool, page_size * n_heads, hw)

    hp = x_prompt.reshape(bp * sp, d)
    hs = x_sample.reshape(bs * ts, d)
    row = lambda a: a.reshape(1, -1).astype(F32)
    outs = {k: [] for k in ("kp", "vp", "cp", "rp", "ks", "vs", "cs", "rs")}
    for l in range(depth):
        lam_init = 0.8 - 0.6 * math.exp(-0.3 * l)
        w_in_b = w_in[l].astype(BF16)
        wg = _gate_weights(w_gx[l], w_ga[l])
        lams = [row(a[l]) for a in (lam_q1, lam_k1, lam_q2, lam_k2)]
        gs = row(g_subln[l])
        rglru_w = (conv_w[l], row(conv_b[l]), wg, row(b_gx[l]), row(b_ga[l]), row(a_param[l]))
        post_w = (b_merge[l], w_out[l].astype(BF16), row(g_ffn[l]), w_ffn_in[l].astype(BF16),
                  w_ffn_out[l].astype(BF16), row(g_ple[l]), w_ple_gate[l].astype(BF16),
                  w_ple_proj[l].astype(BF16), row(g_final))
        last = l == depth - 1

        proj, k_rows, v_rows = _inproj(hp, row(g_mix[l]), w_in_b, tm=256, tn=d, k_col=1, v_col=2, n_heads=n_heads)
        r_p, hl_p = _rglru(proj, 3, 4, jnp.zeros((bp, SUBLANES, lru_w), F32), jnp.zeros((bp, 1, lru_w), F32),
                           *rglru_w, n_seq=bp, seq_len=sp, rows=256, reset_first=True)
        o_p = _attn_prompt(proj, bias_diag, bias_prev, lams, gs, n_seq=bp, seq_len=sp, n_heads=n_heads,
                           hd=hd, t=t_attn, hps=2, nt=2, lam_init=lam_init)
        hp = _post(hp, o_p, r_p, proj, 5, 6, p_prompt[l].reshape(bp * sp, -1), *post_w,
                   tm=256, fc=2816, final_norm=last)
        proj3 = proj.reshape(bp, sp, 7 * d)
        outs["kp"].append(k_rows.reshape(bp, sp, n_heads, hw))
        outs["vp"].append(v_rows.reshape(bp, sp, n_heads, hw))
        outs["cp"].append(proj3[:, sp - (kw - 1):, 3 * d:4 * d])
        outs["rp"].append(hl_p[:, SUBLANES - 1])

        proj, k_rows, v_rows = _inproj(hs, row(g_mix[l]), w_in_b, tm=bs * ts, tn=d, k_col=1, v_col=2,
                                       n_heads=n_heads)
        tail0 = jnp.pad(state_conv[l].astype(F32), ((0, 0), (SUBLANES - (kw - 1), 0), (0, 0)))
        r_s, hl_s = _rglru(proj, 3, 4, tail0, state_rnn[l].astype(F32)[:, None, :],
                           *rglru_w, n_seq=bs, seq_len=ts, rows=ts, reset_first=(past == 0))
        proj3 = proj.reshape(bs, ts, 7 * d)
        k_s = k_rows.reshape(bs, ts, n_heads, hw)
        v_s = v_rows.reshape(bs, ts, n_heads, hw)
        o_s = _attn_sample(page_table, proj, k_rows.reshape(bs, ts * n_heads, hw),
                           v_rows.reshape(bs, ts * n_heads, hw),
                           ck, cv, l * n_pool, mask, mask_last, bias_new, lams, gs,
                           n_seq=bs, t=ts, n_heads=n_heads, hd=hd, pages=pages, group=8, lam_init=lam_init)
        hs = _post(hs, o_s, r_s, proj, 5, 6, p_sample[l].reshape(bs * ts, -1), *post_w,
                   tm=bs * ts, fc=2816, final_norm=last)
        outs["ks"].append(k_s)
        outs["vs"].append(v_s)
        outs["cs"].append(proj3[:, ts - (kw - 1):, 3 * d:4 * d])
        outs["rs"].append(hl_s[:, SUBLANES - 1])

    st = lambda k: jnp.stack(outs[k])
    return (hp.reshape(bp, sp, d), hs.reshape(bs, ts, d),
            st("kp"), st("vp"), st("cp"), st("rp"), st("ks"), st("vs"), st("cs"), st("rs"))
```

```python
import functools
import math

import jax
import jax.numpy as jnp
import numpy as np
from jax import lax
from jax.experimental import pallas as pl
from jax.experimental.pallas import tpu as pltpu

F32 = jnp.float32
BF16 = jnp.bfloat16

EPS = 1e-6
NEG_INF = -1e30
MAX_DISTANCE = 128
LRU_C = 8.0
LOG2E = math.log2(math.e)
LANES = 128
SUBLANES = 8
VMEM_LIMIT = 56 * 1024 * 1024


def _cparams(*sem):
    return pltpu.CompilerParams(dimension_semantics=sem, vmem_limit_bytes=VMEM_LIMIT)


def _rms(x, g):
    return x * lax.rsqrt(jnp.mean(x * x, axis=-1, keepdims=True) + EPS) * g


def _const_spec(shape):
    nd = len(shape)
    return pl.BlockSpec(shape, lambda *_: (0,) * nd, pipeline_mode=pl.Buffered(1))


def _inproj_kernel(x_ref, g_ref, w_ref, o_ref, k_ref, v_ref, *, tn, k_col, v_col, n_heads):
    xn = _rms(x_ref[...], g_ref[...]).astype(BF16)
    tm, hw = xn.shape[0], tn // n_heads
    for j in range(w_ref.shape[1] // tn):
        res = jnp.dot(xn, w_ref[:, j * tn:(j + 1) * tn], preferred_element_type=F32)
        o_ref[:, j * tn:(j + 1) * tn] = res
        for col, dst_ref in ((k_col, k_ref), (v_col, v_ref)):
            if j == col:
                for h in range(n_heads):
                    dst_ref[pl.ds(h, tm, stride=n_heads), :] = res[:, h * hw:(h + 1) * hw]


def _inproj(x, g, w_bf16, tm, tn, k_col, v_col, n_heads):
    n, d = x.shape
    nw = w_bf16.shape[1]
    hw = tn // n_heads
    kern = functools.partial(_inproj_kernel, tn=tn, k_col=k_col, v_col=v_col, n_heads=n_heads)
    return pl.pallas_call(
        kern,
        grid=(n // tm,),
        in_specs=[pl.BlockSpec((tm, d), lambda i: (i, 0)),
                  _const_spec((1, d)), _const_spec((d, nw))],
        out_specs=[pl.BlockSpec((tm, nw), lambda i: (i, 0)),
                   pl.BlockSpec((tm * n_heads, hw), lambda i: (i, 0)),
                   pl.BlockSpec((tm * n_heads, hw), lambda i: (i, 0))],
        out_shape=[jax.ShapeDtypeStruct((n, nw), F32),
                   jax.ShapeDtypeStruct((n * n_heads, hw), F32),
                   jax.ShapeDtypeStruct((n * n_heads, hw), F32)],
        compiler_params=_cparams("parallel"),
        name="inproj",
    )(x, g, w_bf16)


def _rglru_kernel(xr_ref, yr_ref, tail_ref, h0_ref, cw_ref, cb_ref, wg_ref, bgx_ref, bga_ref, ap_ref,
                  r_ref, hlast_ref, win_scr, carry_scr, *, nseq, rows, reset_first):
    s = pl.program_id(1)
    w = xr_ref.shape[-1]

    @pl.when(s == 0)
    def _():
        win_scr[:, :SUBLANES, :] = tail_ref[...]
        carry_scr[...] = h0_ref[...]

    x = xr_ref[...].reshape(nseq, rows, w)
    cw = cw_ref[...]
    kw = cw.shape[0]
    row8 = lax.broadcasted_iota(jnp.int32, (1, SUBLANES, w), 1)
    win_scr[:, SUBLANES:, :] = x
    xc = cb_ref[...] + x * cw[kw - 1:kw]
    for k in range(1, kw):
        xc = xc + win_scr[:, SUBLANES - k:SUBLANES - k + rows, :] * cw[kw - 1 - k:kw - k]
    win_scr[:, :SUBLANES, :] = x[:, rows - SUBLANES:, :]

    xc2 = xc.reshape(nseq * rows, w)
    xcb = xc2.astype(BF16)
    gx_pre, ga_pre = [], []
    for g in range(w // LANES):
        pre = jnp.dot(xcb[:, g * LANES:(g + 1) * LANES], wg_ref[g], preferred_element_type=F32)
        gx_pre.append(pre[:, :LANES])
        ga_pre.append(pre[:, LANES:])
    gx = jax.nn.sigmoid(jnp.concatenate(gx_pre, axis=1) + bgx_ref[...])
    ga = jax.nn.sigmoid(jnp.concatenate(ga_pre, axis=1) + bga_ref[...])

    z = -ap_ref[...]
    softplus = jnp.maximum(z, 0.0) + jnp.log1p(jnp.exp(-jnp.abs(z)))
    log_a = -LRU_C * ga * softplus
    a = jnp.exp(log_a)
    mult = jnp.sqrt(-jnp.tanh(log_a) * (a * a + 1.0))
    if reset_first:
        first = row8 == jnp.where(s == 0, 0, -1)

        def reset(v, value):
            v = v.reshape(nseq, rows, w)
            head = jnp.where(first, value, v[:, :SUBLANES, :])
            v = jnp.concatenate([head, v[:, SUBLANES:, :]], axis=1) if rows > SUBLANES else head
            return v.reshape(nseq * rows, w)

        a = reset(a, 0.0)
        mult = reset(mult, 1.0)
    b = xc2 * gx * mult

    ngrp = nseq * rows // SUBLANES
    av = a.reshape(ngrp, SUBLANES, w)
    bv = b.reshape(ngrp, SUBLANES, w)
    d = 1
    while d < SUBLANES:
        keep = row8 >= d
        a_prev = jnp.where(keep, pltpu.roll(av, d, axis=1), 1.0)
        b_prev = jnp.where(keep, pltpu.roll(bv, d, axis=1), 0.0)
        bv = av * b_prev + bv
        av = av * a_prev
        d *= 2
    av = av.reshape(nseq, rows, w)
    bv = bv.reshape(nseq, rows, w)
    carry = carry_scr[...]
    hs = []
    for g in range(rows // SUBLANES):
        sl = slice(g * SUBLANES, (g + 1) * SUBLANES)
        hg = av[:, sl, :] * carry + bv[:, sl, :]
        carry = hg[:, SUBLANES - 1:, :]
        hs.append(hg)
    carry_scr[...] = carry
    hlast_ref[...] = hs[-1]
    h = jnp.concatenate(hs, axis=1) if len(hs) > 1 else hs[0]
    r_ref[...] = h.reshape(nseq * rows, w) * jax.nn.gelu(yr_ref[...])


def _rglru(proj, col_x, col_y, tail0, h0, cw, cb, wg, bgx, bga, ap, *, n_seq, seq_len, rows, reset_first):
    w = cw.shape[1]
    n = n_seq * seq_len
    if seq_len >= rows:
        assert seq_len % rows == 0
        nseq_t, steps, grid0 = 1, seq_len // rows, n_seq
    else:
        assert seq_len == rows == SUBLANES
        nseq_t, steps, grid0 = n_seq, 1, 1
    tm = nseq_t * rows
    kern = functools.partial(_rglru_kernel, nseq=nseq_t, rows=rows, reset_first=reset_first)
    small = lambda shape: pl.BlockSpec(shape, lambda b, s: (0,) * len(shape))
    return pl.pallas_call(
        kern,
        grid=(grid0, steps),
        in_specs=[pl.BlockSpec((tm, w), lambda b, s: (b * steps + s, col_x)),
                  pl.BlockSpec((tm, w), lambda b, s: (b * steps + s, col_y)),
                  pl.BlockSpec((nseq_t, SUBLANES, w), lambda b, s: (b, 0, 0)),
                  pl.BlockSpec((nseq_t, 1, w), lambda b, s: (b, 0, 0)),
                  small(cw.shape), small(cb.shape), small(wg.shape),
                  small(bgx.shape), small(bga.shape), small(ap.shape)],
        out_specs=[pl.BlockSpec((tm, w), lambda b, s: (b * steps + s, 0)),
                   pl.BlockSpec((nseq_t, SUBLANES, w), lambda b, s: (b, 0, 0))],
        out_shape=[jax.ShapeDtypeStruct((n, w), F32),
                   jax.ShapeDtypeStruct((n_seq, SUBLANES, w), F32)],
        scratch_shapes=[pltpu.VMEM((nseq_t, SUBLANES + rows, w), F32),
                        pltpu.VMEM((nseq_t, 1, w), F32)],
        compiler_params=_cparams("parallel", "arbitrary"),
        name="rglru",
    )(proj, proj, tail0, h0, cw, cb, wg, bgx, bga, ap)


def _lam(lq1, lk1, lq2, lk2, lam_init):
    return (jnp.exp(jnp.sum(lq1[...] * lk1[...], axis=-1, keepdims=True))
            - jnp.exp(jnp.sum(lq2[...] * lk2[...], axis=-1, keepdims=True)) + lam_init)


def _two_map_queries(q, hd):
    lane = lax.broadcasted_iota(jnp.int32, q.shape, 1)
    return jnp.concatenate([jnp.where(lane < hd, q, 0.0), jnp.where(lane >= hd, q, 0.0)], axis=0).astype(BF16)


def _softmax_numerators(s, m):
    m_new = jnp.maximum(m, jnp.max(s, axis=1, keepdims=True))
    return m_new, jnp.exp2(m - m_new), jnp.exp2(s - m_new)


def _online_softmax_step(s, m, l, acc, v):
    m_new, alpha, p = _softmax_numerators(s, m)
    l = alpha * l + jnp.sum(p, axis=1, keepdims=True)
    acc = alpha * acc + jnp.dot(p.astype(BF16), v, preferred_element_type=F32)
    return m_new, l, acc


def _diff_finish(o, t, lam, g, lam_init):
    return _rms(o[:t] - lam * o[t:], g) * (1.0 - lam_init)


_NT = (((1,), (1,)), ((), ()))
_NOT_VISIBLE = object()


def _attn_prompt_kernel(q_ref, k_ref, v_ref, bd_ref, bp_ref, lq1, lk1, lq2, lk2, gs_ref, o_ref,
                        kb_ref, vb_ref, *state, t, hd, hps, nt, scale, lam_init):
    first = pl.program_id(2) * nt
    hw = 2 * hd

    @pl.when(first == 0)
    def _():
        kb_ref[...] = k_ref[...].astype(BF16)
        for g in range(hps):
            vb_ref[:, 2 * g * hw:(2 * g + 1) * hw] = v_ref[:, g * hw:(g + 1) * hw].astype(BF16)
            vb_ref[:, (2 * g + 1) * hw:(2 * g + 2) * hw] = jnp.ones((vb_ref.shape[0], hw), BF16)

    q = q_ref[...] * scale
    lane = lax.broadcasted_iota(jnp.int32, (t, hw), 1)
    chains = []
    for u in range(nt):
        for g in range(hps):
            qg = q[u * t:(u + 1) * t, g * hw:(g + 1) * hw]
            chains.append((u, g, jnp.where(lane < hd, qg, 0.0).astype(BF16)))
            chains.append((u, g, jnp.where(lane >= hd, qg, 0.0).astype(BF16)))

    m_scrs, acc_scrs = state[:len(chains)], state[len(chains):]
    for m_scr, acc_scr in zip(m_scrs, acc_scrs):
        m_scr[...] = jnp.full(m_scr.shape, NEG_INF, F32)
        acc_scr[...] = jnp.zeros(acc_scr.shape, F32)

    sb = bd_ref.shape[-1]
    nsb = t // sb

    def add_near_bias(s, kind, g):
        if kind == "prev":
            top = s[:sb]
            top = jnp.concatenate([top[:, :t - sb], top[:, t - sb:] + bp_ref[g]], axis=1)
            return jnp.concatenate([top, s[sb:]], axis=0)
        rows = []
        for i in range(nsb):
            si = s[i * sb:(i + 1) * sb]
            parts = [si[:, :(i - 1) * sb]] if i >= 2 else []
            if i >= 1:
                parts.append(si[:, (i - 1) * sb:i * sb] + bp_ref[g])
            parts.append(si[:, i * sb:(i + 1) * sb] + bd_ref[g])
            if i < nsb - 1:
                parts.append(jnp.full((sb, (nsb - 1 - i) * sb), NEG_INF, F32))
            rows.append(jnp.concatenate(parts, axis=1))
        return jnp.concatenate(rows, axis=0)

    def chunk(j, kind_of_tile):
        start = pl.multiple_of(j * t, t)
        for (u, g, qz), m_scr, acc_scr in zip(chains, m_scrs, acc_scrs):
            kind = kind_of_tile[u]
            if kind is _NOT_VISIBLE:
                continue
            s = lax.dot_general(qz, kb_ref[pl.ds(start, t), g * hw:(g + 1) * hw], _NT, preferred_element_type=F32)
            if kind is not None:
                s = add_near_bias(s, kind, g)
            m_old = m_scr[...]
            m_new = jnp.maximum(m_old, jnp.max(s, axis=1, keepdims=True))
            alpha = jnp.exp2(m_old - m_new)
            p = jnp.exp2(s - jnp.concatenate([m_new] * (t // LANES), axis=1))
            m_scr[...] = m_new
            acc_scr[...] = jnp.concatenate([alpha] * (2 * hw // LANES), axis=1) * acc_scr[...] + jnp.dot(
                p.astype(BF16), vb_ref[pl.ds(start, t), 2 * g * hw:(2 * g + 2) * hw], preferred_element_type=F32)

    def far_chunk(j, carry):
        chunk(j, [None] * nt)
        return carry

    lax.fori_loop(0, jnp.maximum(first - 1, 0), far_chunk, 0)
    for e in range(nt + 1):
        rel = [e - 1 - u for u in range(nt)]
        kinds = [None if r <= -2 else "prev" if r == -1 else "diag" if r == 0 else _NOT_VISIBLE for r in rel]
        if e == 0:
            pl.when(first >= 1)(functools.partial(chunk, first - 1, kinds))
        else:
            chunk(first + e - 1, kinds)

    lam = _lam(lq1, lk1, lq2, lk2, lam_init)
    for u in range(nt):
        for g in range(hps):
            accs = [acc_scrs[(u * hps + g) * 2 + c] for c in range(2)]
            o = jnp.concatenate([a[:, :hw] / a[:, hw:] for a in accs], axis=0)
            o_ref[u * t:(u + 1) * t, g * hw:(g + 1) * hw] = _diff_finish(o, t, lam, gs_ref[...], lam_init)


def _attn_prompt(proj, bias_diag, bias_prev, lams, gs, *, n_seq, seq_len, n_heads, hd, t, hps, nt, lam_init):
    n, vw = n_seq * seq_len, n_heads * 2 * hd
    nq = seq_len // (t * nt)
    bw = hps * 2 * hd
    ng = n_heads // hps
    kern = functools.partial(_attn_prompt_kernel, t=t, hd=hd, hps=hps, nt=nt, scale=hd ** -0.5 * LOG2E,
                             lam_init=lam_init)
    vec = lambda a: pl.BlockSpec(a.shape, lambda h, b, i: (0, 0))
    return pl.pallas_call(
        kern,
        grid=(ng, n_seq, nq),
        in_specs=[pl.BlockSpec((nt * t, bw), lambda h, b, i: (b * nq + i, h)),
                  pl.BlockSpec((seq_len, bw), lambda h, b, i: (b, ng + h)),
                  pl.BlockSpec((seq_len, bw), lambda h, b, i: (b, 2 * ng + h)),
                  pl.BlockSpec((hps,) + bias_diag.shape[1:], lambda h, b, i: (h, 0, 0)),
                  pl.BlockSpec((hps,) + bias_prev.shape[1:], lambda h, b, i: (h, 0, 0)),
                  *[vec(a) for a in lams], vec(gs)],
        out_specs=pl.BlockSpec((nt * t, bw), lambda h, b, i: (b * nq + i, h)),
        out_shape=jax.ShapeDtypeStruct((n, vw), F32),
        scratch_shapes=[pltpu.VMEM((seq_len, bw), BF16), pltpu.VMEM((seq_len, 2 * bw), BF16),
                        *[pltpu.VMEM((t, LANES), F32)] * (2 * hps * nt),
                        *[pltpu.VMEM((t, 4 * hd), F32)] * (2 * hps * nt)],
        compiler_params=_cparams("parallel", "parallel", "arbitrary"),
        name="attn_prompt",
    )(proj, proj, proj, bias_diag, bias_prev, *lams, gs)


def _attn_sample_kernel(pt_ref, q_ref, kn_ref, vn_ref, *rest, pages, group, n_heads, hd, t, scale, lam_init):
    kp = rest[:pages]
    vp = rest[pages:2 * pages]
    mask_ref, mlast_ref, bn_ref, lq1, lk1, lq2, lk2, gs_ref, o_ref, qr_scr, m_scr, l_scr, acc_scr = rest[2 * pages:]
    del pt_ref
    c = pl.program_id(1)
    hw = 2 * hd

    @pl.when(c == 0)
    def _():
        q = q_ref[...] * scale
        qr_scr[...] = jnp.concatenate(
            [_two_map_queries(q[:, h * hw:(h + 1) * hw], hd) for h in range(n_heads)], axis=0)
        m_scr[...] = jnp.full(m_scr.shape, NEG_INF, F32)
        l_scr[...] = jnp.zeros(l_scr.shape, F32)
        acc_scr[...] = jnp.zeros(acc_scr.shape, F32)

    qrows = qr_scr[...]
    carry = (m_scr[...], l_scr[...], acc_scr[...])
    for i0 in range(0, pages, group):
        m_old, l, acc = carry
        scores = []
        m_new = m_old
        for i in range(i0, i0 + group):
            table = mlast_ref[0] if i == pages - 1 else mask_ref[...]
            s = lax.dot_general(qrows, kp[i][...].astype(BF16), _NT, preferred_element_type=F32) + table
            m_new = jnp.maximum(m_new, jnp.max(s, axis=1, keepdims=True))
            scores.append(s)
        alpha = jnp.exp2(m_old - m_new)
        l = alpha * l
        acc = alpha * acc
        for i, s in zip(range(i0, i0 + group), scores):
            p = jnp.exp2(s - m_new)
            l = l + jnp.sum(p, axis=1, keepdims=True)
            acc = acc + jnp.dot(p.astype(BF16), vp[i][...].astype(BF16), preferred_element_type=F32)
        carry = (m_new, l, acc)
    m_scr[...], l_scr[...], acc_scr[...] = carry

    @pl.when(c == pl.num_programs(1) - 1)
    def _():
        lam = _lam(lq1, lk1, lq2, lk2, lam_init)
        pad = jnp.zeros((qrows.shape[0] - kn_ref.shape[0], hw), F32)
        kn = jnp.concatenate([kn_ref[...], pad], axis=0).astype(BF16)
        vn = jnp.concatenate([vn_ref[...], pad], axis=0).astype(BF16)
        s = lax.dot_general(qrows, kn, _NT, preferred_element_type=F32) + bn_ref[...]
        _, l, acc = _online_softmax_step(s, *carry, vn)
        o = acc / l
        for h in range(n_heads):
            o_ref[:, h * hw:(h + 1) * hw] = _diff_finish(o[h * 2 * t:(h + 1) * 2 * t], t, lam, gs_ref[...], lam_init)


def _attn_sample(page_table, proj, k_new, v_new, cache_k, cache_v, base, mask, mask_last, bias_new, lams, gs,
                 *, n_seq, t, n_heads, hd, pages, group, lam_init):
    hw = 2 * hd
    vw = n_heads * hw
    nrow = n_heads * 2 * t
    n_pages = page_table.shape[1]
    assert n_pages % pages == 0 and nrow == LANES
    n_chunks = n_pages // pages
    page_rows = cache_k.shape[1]
    assert pages % group == 0
    kern = functools.partial(_attn_sample_kernel, pages=pages, group=group, n_heads=n_heads, hd=hd, t=t,
                             scale=hd ** -0.5 * LOG2E, lam_init=lam_init)

    def page_spec(i):
        return pl.BlockSpec((None, page_rows, hw), lambda b, c, pt: (base + pt[b, c * pages + i], 0, 0))

    vec = lambda a: pl.BlockSpec(a.shape, lambda b, c, pt: (0,) * a.ndim)
    grid_spec = pltpu.PrefetchScalarGridSpec(
        num_scalar_prefetch=1,
        grid=(n_seq, n_chunks),
        in_specs=[pl.BlockSpec((t, vw), lambda b, c, pt: (b, 0)),
                  pl.BlockSpec((None, t * n_heads, hw), lambda b, c, pt: (b, 0, 0)),
                  pl.BlockSpec((None, t * n_heads, hw), lambda b, c, pt: (b, 0, 0)),
                  *[page_spec(i) for i in range(pages)],
                  *[page_spec(i) for i in range(pages)],
                  vec(mask),
                  pl.BlockSpec((1,) + mask_last.shape[1:],
                               lambda b, c, pt: (jnp.where(c == n_chunks - 1, 1, 0), 0, 0)),
                  vec(bias_new), *[vec(a) for a in lams], vec(gs)],
        out_specs=pl.BlockSpec((t, vw), lambda b, c, pt: (b, 0)),
        scratch_shapes=[pltpu.VMEM((nrow, hw), BF16),
                        pltpu.VMEM((nrow, 1), F32),
                        pltpu.VMEM((nrow, 1), F32),
                        pltpu.VMEM((nrow, hw), F32)],
    )
    return pl.pallas_call(
        kern,
        grid_spec=grid_spec,
        out_shape=jax.ShapeDtypeStruct((n_seq * t, vw), F32),
        compiler_params=_cparams("parallel", "arbitrary"),
        name="attn_sample",
    )(page_table, proj, k_new, v_new, *([cache_k] * pages), *([cache_v] * pages),
      mask, mask_last, bias_new, *lams, gs)


def _post_kernel(x_ref, o_ref, r_ref, ga_ref, gr_ref, p_ref, bm_ref, wo_ref, gf_ref, wfi_ref, wfo_ref,
                 gp_ref, wpg_ref, wpp_ref, gfin_ref, y_ref, *, d_ff, fc, final_norm):
    bm = bm_ref[...]
    mix = (jax.nn.sigmoid(ga_ref[...] + bm[0:1]) * o_ref[...]
           + jax.nn.sigmoid(gr_ref[...] + bm[1:2]) * r_ref[...])
    h = x_ref[...] + jnp.dot(mix.astype(BF16), wo_ref[...], preferred_element_type=F32)
    hn = _rms(h, gf_ref[...]).astype(BF16)
    acc = jnp.zeros_like(h)
    for c in range(d_ff // fc):
        gate = jnp.dot(hn, wfi_ref[:, c * fc:(c + 1) * fc], preferred_element_type=F32)
        up = jnp.dot(hn, wfi_ref[:, d_ff + c * fc:d_ff + (c + 1) * fc], preferred_element_type=F32)
        act = (jax.nn.silu(gate) * up).astype(BF16)
        acc = acc + jnp.dot(act, wfo_ref[c * fc:(c + 1) * fc, :], preferred_element_type=F32)
    h = h + acc
    hp = _rms(h, gp_ref[...]).astype(BF16)
    gate = jax.nn.sigmoid(jnp.dot(hp, wpg_ref[...], preferred_element_type=F32))
    h = h + gate * jnp.dot(p_ref[...].astype(BF16), wpp_ref[...], preferred_element_type=F32)
    y_ref[...] = _rms(h, gfin_ref[...]) if final_norm else h


def _post(x, o, r, proj, col_ga, col_gr, p, bm, wo, gf, wfi, wfo, gp, wpg, wpp, gfin, *, tm, fc, final_norm):
    n, d = x.shape
    d_ff = wfo.shape[0]
    assert d_ff % fc == 0
    kern = functools.partial(_post_kernel, d_ff=d_ff, fc=fc, final_norm=final_norm)
    tok = lambda col: pl.BlockSpec((tm, d), lambda i: (i, col))
    return pl.pallas_call(
        kern,
        grid=(n // tm,),
        in_specs=[tok(0), tok(0), tok(0), tok(col_ga), tok(col_gr),
                  pl.BlockSpec((tm, p.shape[1]), lambda i: (i, 0)),
                  *[_const_spec(a.shape) for a in (bm, wo, gf, wfi, wfo, gp, wpg, wpp, gfin)]],
        out_specs=pl.BlockSpec((tm, d), lambda i: (i, 0)),
        out_shape=jax.ShapeDtypeStruct((n, d), F32),
        compiler_params=_cparams("parallel"),
        name="post",
    )(x, o, r, proj, proj, p, bm, wo, gf, wfi, wfo, gp, wpg, wpp, gfin)


def _t5_bucket(n, n_buckets):
    max_exact = n_buckets // 2
    nf = jnp.maximum(n, 1).astype(F32)
    large = max_exact + (jnp.log(nf / max_exact) / math.log(MAX_DISTANCE / max_exact)
                         * (n_buckets - max_exact)).astype(jnp.int32)
    large = jnp.minimum(large, n_buckets - 1)
    return jnp.where(n < max_exact, n, large)


def _far_distance(n_buckets):
    n = np.arange(1, 4 * MAX_DISTANCE, dtype=np.int64)
    max_exact = n_buckets // 2
    large = max_exact + (np.log(n.astype(np.float32) / max_exact) / math.log(MAX_DISTANCE / max_exact)
                         * (n_buckets - max_exact)).astype(np.int64)
    bucket = np.where(n < max_exact, n, np.minimum(large, n_buckets - 1))
    not_last = np.nonzero(bucket != n_buckets - 1)[0]
    return int(n[not_last[-1]]) + 2


def _shifted_bias(rel_bias, dist):
    nb, nh = rel_bias.shape
    rel = rel_bias.astype(F32) - rel_bias[nb - 1].astype(F32)
    bucket = _t5_bucket(jnp.maximum(dist, 0), nb)
    b = jnp.zeros((nh,) + dist.shape, F32)
    for k in range(nb - 1):
        b = jnp.where(bucket == k, rel[k].reshape((nh,) + (1,) * dist.ndim), b)
    return jnp.where(dist >= 0, b * LOG2E, NEG_INF)


def _gate_weights(w_gx, w_ga):
    nb, bw, _ = w_gx.shape
    per = LANES // bw

    def blockdiag(wb):
        wb = wb.reshape(nb // per, per, bw, bw)
        eye = jnp.eye(per, dtype=wb.dtype)
        return jnp.einsum('gpij,pq->gpiqj', wb, eye).reshape(nb // per, LANES, LANES)

    return jnp.concatenate([blockdiag(w_gx), blockdiag(w_ga)], axis=-1).astype(BF16)


def kernel(x_prompt, x_sample, p_prompt, p_sample, cache_k, cache_v, state_conv, state_rnn, page_table,
           rel_bias, g_mix, w_in, conv_w, conv_b, w_gx, b_gx, w_ga, b_ga, a_param,
           lam_q1, lam_k1, lam_q2, lam_k2, g_subln, b_merge, w_out,
           g_ffn, w_ffn_in, w_ffn_out, g_ple, w_ple_gate, w_ple_proj, g_final):
    bp, sp, d = x_prompt.shape
    bs, ts, _ = x_sample.shape
    depth, n_pool, page_size, n_heads, hw = cache_k.shape
    hd = hw // 2
    n_pages = page_table.shape[1]
    past = n_pages * page_size
    lru_w = conv_w.shape[2]
    kw = conv_w.shape[1]
    assert d == n_heads * hw == lru_w and w_in.shape[2] == 7 * d and hw == LANES
    assert ts == SUBLANES and kw - 1 <= min(ts, SUBLANES) and sp % 256 == 0

    t_attn = 512
    far = _far_distance(rel_bias.shape[0])
    assert LANES + 1 >= far and page_size + 1 >= far and t_attn % LANES == 0

    r = jnp.arange(LANES, dtype=jnp.int32)
    diag = r[:, None] - r[None, :]
    bias_diag = _shifted_bias(rel_bias, diag)
    bias_prev = _shifted_bias(rel_bias, diag + LANES)
    pages = 16
    tq = jnp.arange(ts, dtype=jnp.int32)
    kk = jnp.arange(page_size, dtype=jnp.int32)
    same_head = jnp.eye(n_heads, dtype=bool)[:, None, None, None, :]

    def head_table(b):
        full = jnp.where(same_head, b[:, None, :, :, None], NEG_INF)
        full = jnp.broadcast_to(full, (n_heads, 2, ts, b.shape[2], n_heads))
        return full.reshape(n_heads * 2 * ts, b.shape[2] * n_heads)

    mask = head_table(jnp.zeros((n_heads, ts, page_size), F32))
    mask_last = jnp.stack([mask, head_table(_shifted_bias(rel_bias, page_size + tq[:, None] - kk[None, :]))])
    bias_new = head_table(_shifted_bias(rel_bias, tq[:, None] - tq[None, :]))
    bias_new = jnp.pad(bias_new, ((0, 0), (0, LANES - ts * n_heads)), constant_values=NEG_INF)

    ck = cache_k.reshape(depth * n_pool, page_size * n_heads, hw)
    cv = cache_v.reshape(depth * n_pool, page_size * n_heads, hw)

    hp = x_prompt.reshape(bp * sp, d)
    hs = x_sample.reshape(bs * ts, d)
    row = lambda a: a.reshape(1, -1).astype(F32)
    outs = {k: [] for k in ("kp", "vp", "cp", "rp", "ks", "vs", "cs", "rs")}
    for l in range(depth):
        lam_init = 0.8 - 0.6 * math.exp(-0.3 * l)
        w_in_b = w_in[l].astype(BF16)
        wg = _gate_weights(w_gx[l], w_ga[l])
        lams = [row(a[l]) for a in (lam_q1, lam_k1, lam_q2, lam_k2)]
        gs = row(g_subln[l])
        rglru_w = (conv_w[l], row(conv_b[l]), wg, row(b_gx[l]), row(b_ga[l]), row(a_param[l]))
        post_w = (b_merge[l], w_out[l].astype(BF16), row(g_ffn[l]), w_ffn_in[l].astype(BF16),
                  w_ffn_out[l].astype(BF16), row(g_ple[l]), w_ple_gate[l].astype(BF16),
                  w_ple_proj[l].astype(BF16), row(g_final))
        last = l == depth - 1

        proj, k_rows, v_rows = _inproj(hp, row(g_mix[l]), w_in_b, tm=256, tn=d, k_col=1, v_col=2, n_heads=n_heads)
        r_p, hl_p = _rglru(proj, 3, 4, jnp.zeros((bp, SUBLANES, lru_w), F32), jnp.zeros((bp, 1, lru_w), F32),
                           *rglru_w, n_seq=bp, seq_len=sp, rows=256, reset_first=True)
        o_p = _attn_prompt(proj, bias_diag, bias_prev, lams, gs, n_seq=bp, seq_len=sp, n_heads=n_heads,
                           hd=hd, t=t_attn, hps=2, nt=2, lam_init=lam_init)
        hp = _post(hp, o_p, r_p, proj, 5, 6, p_prompt[l].reshape(bp * sp, -1), *post_w,
                   tm=256, fc=2816, final_norm=last)
        proj3 = proj.reshape(bp, sp, 7 * d)
        outs["kp"].append(k_rows.reshape(bp, sp, n_heads, hw))
        outs["vp"].append(v_rows.reshape(bp, sp, n_heads, hw))
        outs["cp"].append(proj3[:, sp - (kw - 1):, 3 * d:4 * d])
        outs["rp"].append(hl_p[:, SUBLANES - 1])

        proj, k_rows, v_rows = _inproj(hs, row(g_mix[l]), w_in_b, tm=bs * ts, tn=d, k_col=1, v_col=2,
                                       n_heads=n_heads)
        tail0 = jnp.pad(state_conv[l].astype(F32), ((0, 0), (SUBLANES - (kw - 1), 0), (0, 0)))
        r_s, hl_s = _rglru(proj, 3, 4, tail0, state_rnn[l].astype(F32)[:, None, :],
                           *rglru_w, n_seq=bs, seq_len=ts, rows=ts, reset_first=(past == 0))
        proj3 = proj.reshape(bs, ts, 7 * d)
        k_s = k_rows.reshape(bs, ts, n_heads, hw)
        v_s = v_rows.reshape(bs, ts, n_heads, hw)
        o_s = _attn_sample(page_table, proj, k_rows.reshape(bs, ts * n_heads, hw),
                           v_rows.reshape(bs, ts * n_heads, hw),
                           ck, cv, l * n_pool, mask, mask_last, bias_new, lams, gs,
                           n_seq=bs, t=ts, n_heads=n_heads, hd=hd, pages=pages, group=8, lam_init=lam_init)
        hs = _post(hs, o_s, r_s, proj, 5, 6, p_sample[l].reshape(bs * ts, -1), *post_w,
                   tm=bs * ts, fc=2816, final_norm=last)
        outs["ks"].append(k_s)
        outs["vs"].append(v_s)
        outs["cs"].append(proj3[:, ts - (kw - 1):, 3 * d:4 * d])
        outs["rs"].append(hl_s[:, SUBLANES - 1])

    st = lambda k: jnp.stack(outs[k])
    return (hp.reshape(bp, sp, d), hs.reshape(bs, ts, d),
            st("kp"), st("vp"), st("cp"), st("rp"), st("ks"), st("vs"), st("cs"), st("rs"))
```

```python
import functools
import math
from typing import NamedTuple

import jax
import jax.numpy as jnp
import numpy as np
from jax import lax
from jax.experimental import pallas as pl
from jax.experimental.pallas import tpu as pltpu

F32 = jnp.float32
BF16 = jnp.bfloat16

EPS = 1e-6
NEG_INF = -1e30
MAX_DISTANCE = 128
LRU_C = 8.0
LOG2E = math.log2(math.e)
LANES = 128
SUBLANES = 8
VMEM_LIMIT = 56 * 1024 * 1024


class Tiles(NamedTuple):
    inproj_rows: int = 256
    rglru_rows: int = 256
    attn_t: int = 512
    attn_heads: int = 2
    attn_tiles: int = 2
    pages: int = 16
    page_group: int = 8
    post_rows: int = 256


TILES = Tiles()


def _cparams(*sem):
    return pltpu.CompilerParams(dimension_semantics=sem, vmem_limit_bytes=VMEM_LIMIT)


def _rms(x, g):
    return x * lax.rsqrt(jnp.mean(x * x, axis=-1, keepdims=True) + EPS) * g


def _const_spec(shape):
    nd = len(shape)
    return pl.BlockSpec(shape, lambda *_: (0,) * nd, pipeline_mode=pl.Buffered(1))


def _inproj_kernel(x_ref, g_ref, w_ref, o_ref, k_ref, v_ref, *, tn, k_col, v_col, n_heads):
    xn = _rms(x_ref[...], g_ref[...]).astype(BF16)
    tm, hw = xn.shape[0], tn // n_heads
    for j in range(w_ref.shape[1] // tn):
        res = jnp.dot(xn, w_ref[:, j * tn:(j + 1) * tn], preferred_element_type=F32)
        o_ref[:, j * tn:(j + 1) * tn] = res
        for col, dst_ref in ((k_col, k_ref), (v_col, v_ref)):
            if j == col:
                for h in range(n_heads):
                    dst_ref[pl.ds(h, tm, stride=n_heads), :] = res[:, h * hw:(h + 1) * hw]


def _inproj(x, g, w_bf16, tm, tn, k_col, v_col, n_heads):
    n, d = x.shape
    nw = w_bf16.shape[1]
    hw = tn // n_heads
    kern = functools.partial(_inproj_kernel, tn=tn, k_col=k_col, v_col=v_col, n_heads=n_heads)
    return pl.pallas_call(
        kern,
        grid=(n // tm,),
        in_specs=[pl.BlockSpec((tm, d), lambda i: (i, 0)),
                  _const_spec((1, d)), _const_spec((d, nw))],
        out_specs=[pl.BlockSpec((tm, nw), lambda i: (i, 0)),
                   pl.BlockSpec((tm * n_heads, hw), lambda i: (i, 0)),
                   pl.BlockSpec((tm * n_heads, hw), lambda i: (i, 0))],
        out_shape=[jax.ShapeDtypeStruct((n, nw), F32),
                   jax.ShapeDtypeStruct((n * n_heads, hw), F32),
                   jax.ShapeDtypeStruct((n * n_heads, hw), F32)],
        compiler_params=_cparams("parallel"),
        name="inproj",
    )(x, g, w_bf16)


def _rglru_kernel(xr_ref, yr_ref, tail_ref, h0_ref, cw_ref, cb_ref, wg_ref, bgx_ref, bga_ref, ap_ref,
                  r_ref, hlast_ref, win_scr, carry_scr, *, nseq, rows, reset_first):
    s = pl.program_id(1)
    w = xr_ref.shape[-1]

    @pl.when(s == 0)
    def _():
        win_scr[:, :SUBLANES, :] = tail_ref[...]
        carry_scr[...] = h0_ref[...]

    x = xr_ref[...].reshape(nseq, rows, w)
    cw = cw_ref[...]
    kw = cw.shape[0]
    row8 = lax.broadcasted_iota(jnp.int32, (1, SUBLANES, w), 1)
    win_scr[:, SUBLANES:, :] = x
    xc = cb_ref[...] + x * cw[kw - 1:kw]
    for k in range(1, kw):
        xc = xc + win_scr[:, SUBLANES - k:SUBLANES - k + rows, :] * cw[kw - 1 - k:kw - k]
    win_scr[:, :SUBLANES, :] = x[:, rows - SUBLANES:, :]

    xc2 = xc.reshape(nseq * rows, w)
    xcb = xc2.astype(BF16)
    gx_pre, ga_pre = [], []
    for g in range(w // LANES):
        pre = jnp.dot(xcb[:, g * LANES:(g + 1) * LANES], wg_ref[g], preferred_element_type=F32)
        gx_pre.append(pre[:, :LANES])
        ga_pre.append(pre[:, LANES:])
    gx = jax.nn.sigmoid(jnp.concatenate(gx_pre, axis=1) + bgx_ref[...])
    ga = jax.nn.sigmoid(jnp.concatenate(ga_pre, axis=1) + bga_ref[...])

    z = -ap_ref[...]
    softplus = jnp.maximum(z, 0.0) + jnp.log1p(jnp.exp(-jnp.abs(z)))
    log_a = -LRU_C * ga * softplus
    a = jnp.exp(log_a)
    mult = jnp.sqrt(-jnp.tanh(log_a) * (a * a + 1.0))
    if reset_first:
        first = row8 == jnp.where(s == 0, 0, -1)

        def reset(v, value):
            v = v.reshape(nseq, rows, w)
            head = jnp.where(first, value, v[:, :SUBLANES, :])
            v = jnp.concatenate([head, v[:, SUBLANES:, :]], axis=1) if rows > SUBLANES else head
            return v.reshape(nseq * rows, w)

        a = reset(a, 0.0)
        mult = reset(mult, 1.0)
    b = xc2 * gx * mult

    ngrp = nseq * rows // SUBLANES
    av = a.reshape(ngrp, SUBLANES, w)
    bv = b.reshape(ngrp, SUBLANES, w)
    d = 1
    while d < SUBLANES:
        keep = row8 >= d
        a_prev = jnp.where(keep, pltpu.roll(av, d, axis=1), 1.0)
        b_prev = jnp.where(keep, pltpu.roll(bv, d, axis=1), 0.0)
        bv = av * b_prev + bv
        av = av * a_prev
        d *= 2
    av = av.reshape(nseq, rows, w)
    bv = bv.reshape(nseq, rows, w)
    carry = carry_scr[...]
    hs = []
    for g in range(rows // SUBLANES):
        sl = slice(g * SUBLANES, (g + 1) * SUBLANES)
        hg = av[:, sl, :] * carry + bv[:, sl, :]
        carry = hg[:, SUBLANES - 1:, :]
        hs.append(hg)
    carry_scr[...] = carry
    hlast_ref[...] = hs[-1]
    h = jnp.concatenate(hs, axis=1) if len(hs) > 1 else hs[0]
    r_ref[...] = h.reshape(nseq * rows, w) * jax.nn.gelu(yr_ref[...])


def _rglru(proj, col_x, col_y, tail0, h0, cw, cb, wg, bgx, bga, ap, *, n_seq, seq_len, rows, reset_first):
    w = cw.shape[1]
    n = n_seq * seq_len
    if seq_len >= rows:
        assert seq_len % rows == 0
        nseq_t, steps, grid0 = 1, seq_len // rows, n_seq
    else:
        assert seq_len == rows == SUBLANES
        nseq_t, steps, grid0 = n_seq, 1, 1
    tm = nseq_t * rows
    kern = functools.partial(_rglru_kernel, nseq=nseq_t, rows=rows, reset_first=reset_first)
    small = lambda shape: pl.BlockSpec(shape, lambda b, s: (0,) * len(shape))
    return pl.pallas_call(
        kern,
        grid=(grid0, steps),
        in_specs=[pl.BlockSpec((tm, w), lambda b, s: (b * steps + s, col_x)),
                  pl.BlockSpec((tm, w), lambda b, s: (b * steps + s, col_y)),
                  pl.BlockSpec((nseq_t, SUBLANES, w), lambda b, s: (b, 0, 0)),
                  pl.BlockSpec((nseq_t, 1, w), lambda b, s: (b, 0, 0)),
                  small(cw.shape), small(cb.shape), small(wg.shape),
                  small(bgx.shape), small(bga.shape), small(ap.shape)],
        out_specs=[pl.BlockSpec((tm, w), lambda b, s: (b * steps + s, 0)),
                   pl.BlockSpec((nseq_t, SUBLANES, w), lambda b, s: (b, 0, 0))],
        out_shape=[jax.ShapeDtypeStruct((n, w), F32),
                   jax.ShapeDtypeStruct((n_seq, SUBLANES, w), F32)],
        scratch_shapes=[pltpu.VMEM((nseq_t, SUBLANES + rows, w), F32),
                        pltpu.VMEM((nseq_t, 1, w), F32)],
        compiler_params=_cparams("parallel", "arbitrary"),
        name="rglru",
    )(proj, proj, tail0, h0, cw, cb, wg, bgx, bga, ap)


def _lam(lq1, lk1, lq2, lk2, lam_init):
    return (jnp.exp(jnp.sum(lq1[...] * lk1[...], axis=-1, keepdims=True))
            - jnp.exp(jnp.sum(lq2[...] * lk2[...], axis=-1, keepdims=True)) + lam_init)


def _two_map_queries(q, hd):
    lane = lax.broadcasted_iota(jnp.int32, q.shape, 1)
    return jnp.concatenate([jnp.where(lane < hd, q, 0.0), jnp.where(lane >= hd, q, 0.0)], axis=0).astype(BF16)


def _softmax_numerators(s, m):
    m_new = jnp.maximum(m, jnp.max(s, axis=1, keepdims=True))
    return m_new, jnp.exp2(m - m_new), jnp.exp2(s - m_new)


def _online_softmax_step(s, m, l, acc, v):
    m_new, alpha, p = _softmax_numerators(s, m)
    l = alpha * l + jnp.sum(p, axis=1, keepdims=True)
    acc = alpha * acc + jnp.dot(p.astype(BF16), v, preferred_element_type=F32)
    return m_new, l, acc


def _diff_finish(o, t, lam, g, lam_init):
    return _rms(o[:t] - lam * o[t:], g) * (1.0 - lam_init)


_NT = (((1,), (1,)), ((), ()))
_NOT_VISIBLE = object()


def _attn_prompt_kernel(q_ref, k_ref, v_ref, bd_ref, bp_ref, lq1, lk1, lq2, lk2, gs_ref, o_ref,
                        kb_ref, vb_ref, *state, t, hd, hps, nt, scale, lam_init):
    first = pl.program_id(2) * nt
    hw = 2 * hd

    @pl.when(first == 0)
    def _():
        kb_ref[...] = k_ref[...].astype(BF16)
        for g in range(hps):
            vb_ref[:, 2 * g * hw:(2 * g + 1) * hw] = v_ref[:, g * hw:(g + 1) * hw].astype(BF16)
            vb_ref[:, (2 * g + 1) * hw:(2 * g + 2) * hw] = jnp.ones((vb_ref.shape[0], hw), BF16)

    q = q_ref[...] * scale
    lane = lax.broadcasted_iota(jnp.int32, (t, hw), 1)
    chains = []
    for u in range(nt):
        for g in range(hps):
            qg = q[u * t:(u + 1) * t, g * hw:(g + 1) * hw]
            chains.append((u, g, jnp.where(lane < hd, qg, 0.0).astype(BF16)))
            chains.append((u, g, jnp.where(lane >= hd, qg, 0.0).astype(BF16)))

    m_scrs, acc_scrs = state[:len(chains)], state[len(chains):]
    for m_scr, acc_scr in zip(m_scrs, acc_scrs):
        m_scr[...] = jnp.full(m_scr.shape, NEG_INF, F32)
        acc_scr[...] = jnp.zeros(acc_scr.shape, F32)

    sb = bd_ref.shape[-1]
    nsb = t // sb

    def add_near_bias(s, kind, g):
        if kind == "prev":
            top = s[:sb]
            top = jnp.concatenate([top[:, :t - sb], top[:, t - sb:] + bp_ref[g]], axis=1)
            return jnp.concatenate([top, s[sb:]], axis=0)
        rows = []
        for i in range(nsb):
            si = s[i * sb:(i + 1) * sb]
            parts = [si[:, :(i - 1) * sb]] if i >= 2 else []
            if i >= 1:
                parts.append(si[:, (i - 1) * sb:i * sb] + bp_ref[g])
            parts.append(si[:, i * sb:(i + 1) * sb] + bd_ref[g])
            if i < nsb - 1:
                parts.append(jnp.full((sb, (nsb - 1 - i) * sb), NEG_INF, F32))
            rows.append(jnp.concatenate(parts, axis=1))
        return jnp.concatenate(rows, axis=0)

    def chunk(j, kind_of_tile):
        start = pl.multiple_of(j * t, t)
        for (u, g, qz), m_scr, acc_scr in zip(chains, m_scrs, acc_scrs):
            kind = kind_of_tile[u]
            if kind is _NOT_VISIBLE:
                continue
            s = lax.dot_general(qz, kb_ref[pl.ds(start, t), g * hw:(g + 1) * hw], _NT, preferred_element_type=F32)
            if kind is not None:
                s = add_near_bias(s, kind, g)
            m_old = m_scr[...]
            m_new = jnp.maximum(m_old, jnp.max(s, axis=1, keepdims=True))
            alpha = jnp.exp2(m_old - m_new)
            p = jnp.exp2(s - jnp.concatenate([m_new] * (t // LANES), axis=1))
            m_scr[...] = m_new
            acc_scr[...] = jnp.concatenate([alpha] * (2 * hw // LANES), axis=1) * acc_scr[...] + jnp.dot(
                p.astype(BF16), vb_ref[pl.ds(start, t), 2 * g * hw:(2 * g + 2) * hw], preferred_element_type=F32)

    def far_chunk(j, carry):
        chunk(j, [None] * nt)
        return carry

    lax.fori_loop(0, jnp.maximum(first - 1, 0), far_chunk, 0)
    for e in range(nt + 1):
        rel = [e - 1 - u for u in range(nt)]
        kinds = [None if r <= -2 else "prev" if r == -1 else "diag" if r == 0 else _NOT_VISIBLE for r in rel]
        if e == 0:
            pl.when(first >= 1)(functools.partial(chunk, first - 1, kinds))
        else:
            chunk(first + e - 1, kinds)

    lam = _lam(lq1, lk1, lq2, lk2, lam_init)
    for u in range(nt):
        for g in range(hps):
            accs = [acc_scrs[(u * hps + g) * 2 + c] for c in range(2)]
            o = jnp.concatenate([a[:, :hw] / a[:, hw:] for a in accs], axis=0)
            o_ref[u * t:(u + 1) * t, g * hw:(g + 1) * hw] = _diff_finish(o, t, lam, gs_ref[...], lam_init)


def _attn_prompt(proj, bias_diag, bias_prev, lams, gs, *, n_seq, seq_len, n_heads, hd, t, hps, nt, lam_init):
    n, vw = n_seq * seq_len, n_heads * 2 * hd
    nq = seq_len // (t * nt)
    bw = hps * 2 * hd
    ng = n_heads // hps
    assert t % bias_diag.shape[-1] == 0
    kern = functools.partial(_attn_prompt_kernel, t=t, hd=hd, hps=hps, nt=nt, scale=hd ** -0.5 * LOG2E,
                             lam_init=lam_init)
    vec = lambda a: pl.BlockSpec(a.shape, lambda h, b, i: (0, 0))
    return pl.pallas_call(
        kern,
        grid=(ng, n_seq, nq),
        in_specs=[pl.BlockSpec((nt * t, bw), lambda h, b, i: (b * nq + i, h)),
                  pl.BlockSpec((seq_len, bw), lambda h, b, i: (b, ng + h)),
                  pl.BlockSpec((seq_len, bw), lambda h, b, i: (b, 2 * ng + h)),
                  pl.BlockSpec((hps,) + bias_diag.shape[1:], lambda h, b, i: (h, 0, 0)),
                  pl.BlockSpec((hps,) + bias_prev.shape[1:], lambda h, b, i: (h, 0, 0)),
                  *[vec(a) for a in lams], vec(gs)],
        out_specs=pl.BlockSpec((nt * t, bw), lambda h, b, i: (b * nq + i, h)),
        out_shape=jax.ShapeDtypeStruct((n, vw), F32),
        scratch_shapes=[pltpu.VMEM((seq_len, bw), BF16), pltpu.VMEM((seq_len, 2 * bw), BF16),
                        *[pltpu.VMEM((t, LANES), F32)] * (2 * hps * nt),
                        *[pltpu.VMEM((t, 4 * hd), F32)] * (2 * hps * nt)],
        compiler_params=_cparams("parallel", "parallel", "arbitrary"),
        name="attn_prompt",
    )(proj, proj, proj, bias_diag, bias_prev, *lams, gs)


def _attn_sample_kernel(pt_ref, q_ref, kn_ref, vn_ref, *rest, pages, group, n_heads, hd, t, scale, lam_init):
    kp = rest[:pages]
    vp = rest[pages:2 * pages]
    mask_ref, mlast_ref, bn_ref, lq1, lk1, lq2, lk2, gs_ref, o_ref, qr_scr, m_scr, l_scr, acc_scr = rest[2 * pages:]
    del pt_ref
    c = pl.program_id(1)
    hw = 2 * hd

    @pl.when(c == 0)
    def _():
        q = q_ref[...] * scale
        qr_scr[...] = jnp.concatenate(
            [_two_map_queries(q[:, h * hw:(h + 1) * hw], hd) for h in range(n_heads)], axis=0)
        m_scr[...] = jnp.full(m_scr.shape, NEG_INF, F32)
        l_scr[...] = jnp.zeros(l_scr.shape, F32)
        acc_scr[...] = jnp.zeros(acc_scr.shape, F32)

    qrows = qr_scr[...]
    carry = (m_scr[...], l_scr[...], acc_scr[...])
    for i0 in range(0, pages, group):
        m_old, l, acc = carry
        scores = []
        m_new = m_old
        for i in range(i0, i0 + group):
            table = mlast_ref[0] if i == pages - 1 else mask_ref[...]
            s = lax.dot_general(qrows, kp[i][...].astype(BF16), _NT, preferred_element_type=F32) + table
            m_new = jnp.maximum(m_new, jnp.max(s, axis=1, keepdims=True))
            scores.append(s)
        alpha = jnp.exp2(m_old - m_new)
        l = alpha * l
        acc = alpha * acc
        for i, s in zip(range(i0, i0 + group), scores):
            p = jnp.exp2(s - m_new)
            l = l + jnp.sum(p, axis=1, keepdims=True)
            acc = acc + jnp.dot(p.astype(BF16), vp[i][...].astype(BF16), preferred_element_type=F32)
        carry = (m_new, l, acc)
    m_scr[...], l_scr[...], acc_scr[...] = carry

    @pl.when(c == pl.num_programs(1) - 1)
    def _():
        lam = _lam(lq1, lk1, lq2, lk2, lam_init)
        pad = jnp.zeros((qrows.shape[0] - kn_ref.shape[0], hw), F32)
        kn = jnp.concatenate([kn_ref[...], pad], axis=0).astype(BF16)
        vn = jnp.concatenate([vn_ref[...], pad], axis=0).astype(BF16)
        s = lax.dot_general(qrows, kn, _NT, preferred_element_type=F32) + bn_ref[...]
        _, l, acc = _online_softmax_step(s, *carry, vn)
        o = acc / l
        for h in range(n_heads):
            o_ref[:, h * hw:(h + 1) * hw] = _diff_finish(o[h * 2 * t:(h + 1) * 2 * t], t, lam, gs_ref[...], lam_init)


def _attn_sample(page_table, proj, k_new, v_new, cache_k, cache_v, base, mask, mask_last, bias_new, lams, gs,
                 *, n_seq, t, n_heads, hd, pages, group, lam_init):
    hw = 2 * hd
    vw = n_heads * hw
    nrow = n_heads * 2 * t
    n_pages = page_table.shape[1]
    assert n_pages % pages == 0 and nrow == LANES
    n_chunks = n_pages // pages
    page_rows = cache_k.shape[1]
    assert pages % group == 0
    kern = functools.partial(_attn_sample_kernel, pages=pages, group=group, n_heads=n_heads, hd=hd, t=t,
                             scale=hd ** -0.5 * LOG2E, lam_init=lam_init)

    def page_spec(i):
        return pl.BlockSpec((None, page_rows, hw), lambda b, c, pt: (base + pt[b, c * pages + i], 0, 0))

    vec = lambda a: pl.BlockSpec(a.shape, lambda b, c, pt: (0,) * a.ndim)
    grid_spec = pltpu.PrefetchScalarGridSpec(
        num_scalar_prefetch=1,
        grid=(n_seq, n_chunks),
        in_specs=[pl.BlockSpec((t, vw), lambda b, c, pt: (b, 0)),
                  pl.BlockSpec((None, t * n_heads, hw), lambda b, c, pt: (b, 0, 0)),
                  pl.BlockSpec((None, t * n_heads, hw), lambda b, c, pt: (b, 0, 0)),
                  *[page_spec(i) for i in range(pages)],
                  *[page_spec(i) for i in range(pages)],
                  vec(mask),
                  pl.BlockSpec((1,) + mask_last.shape[1:],
                               lambda b, c, pt: (jnp.where(c == n_chunks - 1, 1, 0), 0, 0)),
                  vec(bias_new), *[vec(a) for a in lams], vec(gs)],
        out_specs=pl.BlockSpec((t, vw), lambda b, c, pt: (b, 0)),
        scratch_shapes=[pltpu.VMEM((nrow, hw), BF16),
                        pltpu.VMEM((nrow, 1), F32),
                        pltpu.VMEM((nrow, 1), F32),
                        pltpu.VMEM((nrow, hw), F32)],
    )
    return pl.pallas_call(
        kern,
        grid_spec=grid_spec,
        out_shape=jax.ShapeDtypeStruct((n_seq * t, vw), F32),
        compiler_params=_cparams("parallel", "arbitrary"),
        name="attn_sample",
    )(page_table, proj, k_new, v_new, *([cache_k] * pages), *([cache_v] * pages),
      mask, mask_last, bias_new, *lams, gs)


def _post_kernel(x_ref, o_ref, r_ref, ga_ref, gr_ref, p_ref, bm_ref, wo_ref, gf_ref, wfi_ref, wfo_ref,
                 gp_ref, wpg_ref, wpp_ref, gfin_ref, y_ref, *, d_ff, final_norm):
    bm = bm_ref[...]
    mix = (jax.nn.sigmoid(ga_ref[...] + bm[0:1]) * o_ref[...]
           + jax.nn.sigmoid(gr_ref[...] + bm[1:2]) * r_ref[...])
    h = x_ref[...] + jnp.dot(mix.astype(BF16), wo_ref[...], preferred_element_type=F32)
    hn = _rms(h, gf_ref[...]).astype(BF16)
    gate = jnp.dot(hn, wfi_ref[:, :d_ff], preferred_element_type=F32)
    up = jnp.dot(hn, wfi_ref[:, d_ff:], preferred_element_type=F32)
    act = (jax.nn.silu(gate) * up).astype(BF16)
    h = h + jnp.dot(act, wfo_ref[...], preferred_element_type=F32)
    hp = _rms(h, gp_ref[...]).astype(BF16)
    gate = jax.nn.sigmoid(jnp.dot(hp, wpg_ref[...], preferred_element_type=F32))
    h = h + gate * jnp.dot(p_ref[...].astype(BF16), wpp_ref[...], preferred_element_type=F32)
    y_ref[...] = _rms(h, gfin_ref[...]) if final_norm else h


def _post(x, o, r, proj, col_ga, col_gr, p, bm, wo, gf, wfi, wfo, gp, wpg, wpp, gfin, *, tm, final_norm):
    n, d = x.shape
    kern = functools.partial(_post_kernel, d_ff=wfo.shape[0], final_norm=final_norm)
    tok = lambda col: pl.BlockSpec((tm, d), lambda i: (i, col))
    return pl.pallas_call(
        kern,
        grid=(n // tm,),
        in_specs=[tok(0), tok(0), tok(0), tok(col_ga), tok(col_gr),
                  pl.BlockSpec((tm, p.shape[1]), lambda i: (i, 0)),
                  *[_const_spec(a.shape) for a in (bm, wo, gf, wfi, wfo, gp, wpg, wpp, gfin)]],
        out_specs=pl.BlockSpec((tm, d), lambda i: (i, 0)),
        out_shape=jax.ShapeDtypeStruct((n, d), F32),
        compiler_params=_cparams("parallel"),
        name="post",
    )(x, o, r, proj, proj, p, bm, wo, gf, wfi, wfo, gp, wpg, wpp, gfin)


def _t5_bucket(n, n_buckets):
    max_exact = n_buckets // 2
    nf = jnp.maximum(n, 1).astype(F32)
    large = max_exact + (jnp.log(nf / max_exact) / math.log(MAX_DISTANCE / max_exact)
                         * (n_buckets - max_exact)).astype(jnp.int32)
    large = jnp.minimum(large, n_buckets - 1)
    return jnp.where(n < max_exact, n, large)


def _far_distance(n_buckets):
    n = np.arange(1, 4 * MAX_DISTANCE, dtype=np.int64)
    max_exact = n_buckets // 2
    large = max_exact + (np.log(n.astype(np.float32) / max_exact) / math.log(MAX_DISTANCE / max_exact)
                         * (n_buckets - max_exact)).astype(np.int64)
    bucket = np.where(n < max_exact, n, np.minimum(large, n_buckets - 1))
    not_last = np.nonzero(bucket != n_buckets - 1)[0]
    return int(n[not_last[-1]]) + 2


def _shifted_bias(rel_bias, dist):
    nb, nh = rel_bias.shape
    rel = rel_bias.astype(F32) - rel_bias[nb - 1].astype(F32)
    bucket = _t5_bucket(jnp.maximum(dist, 0), nb)
    b = jnp.zeros((nh,) + dist.shape, F32)
    for k in range(nb - 1):
        b = jnp.where(bucket == k, rel[k].reshape((nh,) + (1,) * dist.ndim), b)
    return jnp.where(dist >= 0, b * LOG2E, NEG_INF)


def _gate_weights(w_gx, w_ga):
    nb, bw, _ = w_gx.shape
    per = LANES // bw

    def blockdiag(wb):
        wb = wb.reshape(nb // per, per, bw, bw)
        eye = jnp.eye(per, dtype=wb.dtype)
        return jnp.einsum('gpij,pq->gpiqj', wb, eye).reshape(nb // per, LANES, LANES)

    return jnp.concatenate([blockdiag(w_gx), blockdiag(w_ga)], axis=-1).astype(BF16)


def kernel(x_prompt, x_sample, p_prompt, p_sample, cache_k, cache_v, state_conv, state_rnn, page_table,
           rel_bias, g_mix, w_in, conv_w, conv_b, w_gx, b_gx, w_ga, b_ga, a_param,
           lam_q1, lam_k1, lam_q2, lam_k2, g_subln, b_merge, w_out,
           g_ffn, w_ffn_in, w_ffn_out, g_ple, w_ple_gate, w_ple_proj, g_final):
    bp, sp, d = x_prompt.shape
    bs, ts, _ = x_sample.shape
    depth, n_pool, page_size, n_heads, hw = cache_k.shape
    hd = hw // 2
    n_pages = page_table.shape[1]
    past = n_pages * page_size
    lru_w = conv_w.shape[2]
    kw = conv_w.shape[1]
    tl = TILES
    assert d == n_heads * hw == lru_w and w_in.shape[2] == 7 * d and hw == LANES
    assert ts == SUBLANES and kw - 1 <= min(ts, SUBLANES)

    far = _far_distance(rel_bias.shape[0])
    assert LANES + 1 >= far and page_size + 1 >= far

    r = jnp.arange(LANES, dtype=jnp.int32)
    diag = r[:, None] - r[None, :]
    bias_diag = _shifted_bias(rel_bias, diag)
    bias_prev = _shifted_bias(rel_bias, diag + LANES)
    tq = jnp.arange(ts, dtype=jnp.int32)
    kk = jnp.arange(page_size, dtype=jnp.int32)
    same_head = jnp.eye(n_heads, dtype=bool)[:, None, None, None, :]

    def head_table(b):
        full = jnp.where(same_head, b[:, None, :, :, None], NEG_INF)
        full = jnp.broadcast_to(full, (n_heads, 2, ts, b.shape[2], n_heads))
        return full.reshape(n_heads * 2 * ts, b.shape[2] * n_heads)

    mask = head_table(jnp.zeros((n_heads, ts, page_size), F32))
    mask_last = jnp.stack([mask, head_table(_shifted_bias(rel_bias, page_size + tq[:, None] - kk[None, :]))])
    bias_new = head_table(_shifted_bias(rel_bias, tq[:, None] - tq[None, :]))
    bias_new = jnp.pad(bias_new, ((0, 0), (0, LANES - ts * n_heads)), constant_values=NEG_INF)

    ck = cache_k.reshape(depth * n_pool, page_size * n_heads, hw)
    cv = cache_v.reshape(depth * n_pool, page_size * n_heads, hw)

    hp = x_prompt.reshape(bp * sp, d)
    hs = x_sample.reshape(bs * ts, d)
    row = lambda a: a.reshape(1, -1).astype(F32)
    outs = {k: [] for k in ("kp", "vp", "cp", "rp", "ks", "vs", "cs", "rs")}
    for l in range(depth):
        lam_init = 0.8 - 0.6 * math.exp(-0.3 * l)
        w_in_b = w_in[l].astype(BF16)
        wg = _gate_weights(w_gx[l], w_ga[l])
        lams = [row(a[l]) for a in (lam_q1, lam_k1, lam_q2, lam_k2)]
        gs = row(g_subln[l])
        rglru_w = (conv_w[l], row(conv_b[l]), wg, row(b_gx[l]), row(b_ga[l]), row(a_param[l]))
        post_w = (b_merge[l], w_out[l].astype(BF16), row(g_ffn[l]), w_ffn_in[l].astype(BF16),
                  w_ffn_out[l].astype(BF16), row(g_ple[l]), w_ple_gate[l].astype(BF16),
                  w_ple_proj[l].astype(BF16), row(g_final))
        last = l == depth - 1

        proj, k_rows, v_rows = _inproj(hp, row(g_mix[l]), w_in_b, tm=tl.inproj_rows, tn=d, k_col=1, v_col=2,
                                       n_heads=n_heads)
        r_p, hl_p = _rglru(proj, 3, 4, jnp.zeros((bp, SUBLANES, lru_w), F32), jnp.zeros((bp, 1, lru_w), F32),
                           *rglru_w, n_seq=bp, seq_len=sp, rows=tl.rglru_rows, reset_first=True)
        o_p = _attn_prompt(proj, bias_diag, bias_prev, lams, gs, n_seq=bp, seq_len=sp, n_heads=n_heads,
                           hd=hd, t=tl.attn_t, hps=tl.attn_heads, nt=tl.attn_tiles, lam_init=lam_init)
        hp = _post(hp, o_p, r_p, proj, 5, 6, p_prompt[l].reshape(bp * sp, -1), *post_w,
                   tm=tl.post_rows, final_norm=last)
        proj3 = proj.reshape(bp, sp, 7 * d)
        outs["kp"].append(k_rows.reshape(bp, sp, n_heads, hw))
        outs["vp"].append(v_rows.reshape(bp, sp, n_heads, hw))
        outs["cp"].append(proj3[:, sp - (kw - 1):, 3 * d:4 * d])
        outs["rp"].append(hl_p[:, SUBLANES - 1])

        proj, k_rows, v_rows = _inproj(hs, row(g_mix[l]), w_in_b, tm=bs * ts, tn=d, k_col=1, v_col=2,
                                       n_heads=n_heads)
        tail0 = jnp.pad(state_conv[l].astype(F32), ((0, 0), (SUBLANES - (kw - 1), 0), (0, 0)))
        r_s, hl_s = _rglru(proj, 3, 4, tail0, state_rnn[l].astype(F32)[:, None, :],
                           *rglru_w, n_seq=bs, seq_len=ts, rows=ts, reset_first=(past == 0))
        proj3 = proj.reshape(bs, ts, 7 * d)
        k_s = k_rows.reshape(bs, ts, n_heads, hw)
        v_s = v_rows.reshape(bs, ts, n_heads, hw)
        o_s = _attn_sample(page_table, proj, k_rows.reshape(bs, ts * n_heads, hw),
                           v_rows.reshape(bs, ts * n_heads, hw),
                           ck, cv, l * n_pool, mask, mask_last, bias_new, lams, gs,
                           n_seq=bs, t=ts, n_heads=n_heads, hd=hd, pages=tl.pages, group=tl.page_group,
                           lam_init=lam_init)
        hs = _post(hs, o_s, r_s, proj, 5, 6, p_sample[l].reshape(bs * ts, -1), *post_w,
                   tm=bs * ts, final_norm=last)
        outs["ks"].append(k_s)
        outs["vs"].append(v_s)
        outs["cs"].append(proj3[:, ts - (kw - 1):, 3 * d:4 * d])
        outs["rs"].append(hl_s[:, SUBLANES - 1])

    st = lambda k: jnp.stack(outs[k])
    return (hp.reshape(bp, sp, d), hs.reshape(bs, ts, d),
            st("kp"), st("vp"), st("cp"), st("rp"), st("ks"), st("vs"), st("cs"), st("rs"))
```

```python
import functools
import math
from typing import NamedTuple

import jax
import jax.numpy as jnp
import numpy as np
from jax import lax
from jax.experimental import pallas as pl
from jax.experimental.pallas import tpu as pltpu

F32 = jnp.float32
BF16 = jnp.bfloat16

EPS = 1e-6
NEG_INF = -1e30
MAX_DISTANCE = 128
LRU_C = 8.0
LOG2E = math.log2(math.e)
LANES = 128
SUBLANES = 8
VMEM_LIMIT = 56 * 1024 * 1024


class Tiles(NamedTuple):
    inproj_rows: int = 256
    rglru_rows: int = 256
    attn_t: int = 512
    attn_heads: int = 2
    attn_tiles: int = 2
    pages: int = 8
    page_group: int = 8
    page_ring: int = 3
    post_rows: int = 256


TILES = Tiles()


def _cparams(*sem):
    return pltpu.CompilerParams(dimension_semantics=sem, vmem_limit_bytes=VMEM_LIMIT)


def _rms(x, g):
    return x * lax.rsqrt(jnp.mean(x * x, axis=-1, keepdims=True) + EPS) * g


def _const_spec(shape):
    nd = len(shape)
    return pl.BlockSpec(shape, lambda *_: (0,) * nd, pipeline_mode=pl.Buffered(1))


def _inproj_kernel(x_ref, g_ref, w_ref, o_ref, k_ref, v_ref, *, tn, k_col, v_col, n_heads):
    xn = _rms(x_ref[...], g_ref[...]).astype(BF16)
    tm, hw = xn.shape[0], tn // n_heads
    for j in range(w_ref.shape[1] // tn):
        res = jnp.dot(xn, w_ref[:, j * tn:(j + 1) * tn], preferred_element_type=F32)
        o_ref[:, j * tn:(j + 1) * tn] = res
        for col, dst_ref in ((k_col, k_ref), (v_col, v_ref)):
            if j == col:
                for h in range(n_heads):
                    dst_ref[pl.ds(h, tm, stride=n_heads), :] = res[:, h * hw:(h + 1) * hw]


def _inproj(x, g, w_bf16, tm, tn, k_col, v_col, n_heads):
    n, d = x.shape
    nw = w_bf16.shape[1]
    hw = tn // n_heads
    kern = functools.partial(_inproj_kernel, tn=tn, k_col=k_col, v_col=v_col, n_heads=n_heads)
    return pl.pallas_call(
        kern,
        grid=(n // tm,),
        in_specs=[pl.BlockSpec((tm, d), lambda i: (i, 0)),
                  _const_spec((1, d)), _const_spec((d, nw))],
        out_specs=[pl.BlockSpec((tm, nw), lambda i: (i, 0)),
                   pl.BlockSpec((tm * n_heads, hw), lambda i: (i, 0)),
                   pl.BlockSpec((tm * n_heads, hw), lambda i: (i, 0))],
        out_shape=[jax.ShapeDtypeStruct((n, nw), F32),
                   jax.ShapeDtypeStruct((n * n_heads, hw), F32),
                   jax.ShapeDtypeStruct((n * n_heads, hw), F32)],
        compiler_params=_cparams("parallel"),
        name="inproj",
    )(x, g, w_bf16)


def _rglru_kernel(xr_ref, yr_ref, tail_ref, h0_ref, cw_ref, cb_ref, wg_ref, bgx_ref, bga_ref, ap_ref,
                  r_ref, hlast_ref, win_scr, carry_scr, *, nseq, rows, reset_first):
    s = pl.program_id(1)
    w = xr_ref.shape[-1]

    @pl.when(s == 0)
    def _():
        win_scr[:, :SUBLANES, :] = tail_ref[...]
        carry_scr[...] = h0_ref[...]

    x = xr_ref[...].reshape(nseq, rows, w)
    cw = cw_ref[...]
    kw = cw.shape[0]
    row8 = lax.broadcasted_iota(jnp.int32, (1, SUBLANES, w), 1)
    win_scr[:, SUBLANES:, :] = x
    xc = cb_ref[...] + x * cw[kw - 1:kw]
    for k in range(1, kw):
        xc = xc + win_scr[:, SUBLANES - k:SUBLANES - k + rows, :] * cw[kw - 1 - k:kw - k]
    win_scr[:, :SUBLANES, :] = x[:, rows - SUBLANES:, :]

    xc2 = xc.reshape(nseq * rows, w)
    xcb = xc2.astype(BF16)
    gx_pre, ga_pre = [], []
    for g in range(w // LANES):
        pre = jnp.dot(xcb[:, g * LANES:(g + 1) * LANES], wg_ref[g], preferred_element_type=F32)
        gx_pre.append(pre[:, :LANES])
        ga_pre.append(pre[:, LANES:])
    gx = jax.nn.sigmoid(jnp.concatenate(gx_pre, axis=1) + bgx_ref[...])
    ga = jax.nn.sigmoid(jnp.concatenate(ga_pre, axis=1) + bga_ref[...])

    z = -ap_ref[...]
    softplus = jnp.maximum(z, 0.0) + jnp.log1p(jnp.exp(-jnp.abs(z)))
    log_a = -LRU_C * ga * softplus
    a = jnp.exp(log_a)
    mult = jnp.sqrt(-jnp.tanh(log_a) * (a * a + 1.0))
    if reset_first:
        first = row8 == jnp.where(s == 0, 0, -1)

        def reset(v, value):
            v = v.reshape(nseq, rows, w)
            head = jnp.where(first, value, v[:, :SUBLANES, :])
            v = jnp.concatenate([head, v[:, SUBLANES:, :]], axis=1) if rows > SUBLANES else head
            return v.reshape(nseq * rows, w)

        a = reset(a, 0.0)
        mult = reset(mult, 1.0)
    b = xc2 * gx * mult

    ngrp = nseq * rows // SUBLANES
    av = a.reshape(ngrp, SUBLANES, w)
    bv = b.reshape(ngrp, SUBLANES, w)
    d = 1
    while d < SUBLANES:
        keep = row8 >= d
        a_prev = jnp.where(keep, pltpu.roll(av, d, axis=1), 1.0)
        b_prev = jnp.where(keep, pltpu.roll(bv, d, axis=1), 0.0)
        bv = av * b_prev + bv
        av = av * a_prev
        d *= 2
    av = av.reshape(nseq, rows, w)
    bv = bv.reshape(nseq, rows, w)
    carry = carry_scr[...]
    hs = []
    for g in range(rows // SUBLANES):
        sl = slice(g * SUBLANES, (g + 1) * SUBLANES)
        hg = av[:, sl, :] * carry + bv[:, sl, :]
        carry = hg[:, SUBLANES - 1:, :]
        hs.append(hg)
    carry_scr[...] = carry
    hlast_ref[...] = hs[-1]
    h = jnp.concatenate(hs, axis=1) if len(hs) > 1 else hs[0]
    r_ref[...] = h.reshape(nseq * rows, w) * jax.nn.gelu(yr_ref[...])


def _rglru(proj, col_x, col_y, tail0, h0, cw, cb, wg, bgx, bga, ap, *, n_seq, seq_len, rows, reset_first):
    w = cw.shape[1]
    n = n_seq * seq_len
    if seq_len >= rows:
        assert seq_len % rows == 0
        nseq_t, steps, grid0 = 1, seq_len // rows, n_seq
    else:
        assert seq_len == rows == SUBLANES
        nseq_t, steps, grid0 = n_seq, 1, 1
    tm = nseq_t * rows
    kern = functools.partial(_rglru_kernel, nseq=nseq_t, rows=rows, reset_first=reset_first)
    small = lambda shape: pl.BlockSpec(shape, lambda b, s: (0,) * len(shape))
    return pl.pallas_call(
        kern,
        grid=(grid0, steps),
        in_specs=[pl.BlockSpec((tm, w), lambda b, s: (b * steps + s, col_x)),
                  pl.BlockSpec((tm, w), lambda b, s: (b * steps + s, col_y)),
                  pl.BlockSpec((nseq_t, SUBLANES, w), lambda b, s: (b, 0, 0)),
                  pl.BlockSpec((nseq_t, 1, w), lambda b, s: (b, 0, 0)),
                  small(cw.shape), small(cb.shape), small(wg.shape),
                  small(bgx.shape), small(bga.shape), small(ap.shape)],
        out_specs=[pl.BlockSpec((tm, w), lambda b, s: (b * steps + s, 0)),
                   pl.BlockSpec((nseq_t, SUBLANES, w), lambda b, s: (b, 0, 0))],
        out_shape=[jax.ShapeDtypeStruct((n, w), F32),
                   jax.ShapeDtypeStruct((n_seq, SUBLANES, w), F32)],
        scratch_shapes=[pltpu.VMEM((nseq_t, SUBLANES + rows, w), F32),
                        pltpu.VMEM((nseq_t, 1, w), F32)],
        compiler_params=_cparams("parallel", "arbitrary"),
        name="rglru",
    )(proj, proj, tail0, h0, cw, cb, wg, bgx, bga, ap)


def _lam(lq1, lk1, lq2, lk2, lam_init):
    return (jnp.exp(jnp.sum(lq1[...] * lk1[...], axis=-1, keepdims=True))
            - jnp.exp(jnp.sum(lq2[...] * lk2[...], axis=-1, keepdims=True)) + lam_init)


def _two_map_queries(q, hd):
    lane = lax.broadcasted_iota(jnp.int32, q.shape, 1)
    return jnp.concatenate([jnp.where(lane < hd, q, 0.0), jnp.where(lane >= hd, q, 0.0)], axis=0).astype(BF16)


def _softmax_numerators(s, m):
    m_new = jnp.maximum(m, jnp.max(s, axis=1, keepdims=True))
    return m_new, jnp.exp2(m - m_new), jnp.exp2(s - m_new)


def _online_softmax_step(s, m, l, acc, v):
    m_new, alpha, p = _softmax_numerators(s, m)
    l = alpha * l + jnp.sum(p, axis=1, keepdims=True)
    acc = alpha * acc + jnp.dot(p.astype(BF16), v, preferred_element_type=F32)
    return m_new, l, acc


def _diff_finish(o, t, lam, g, lam_init):
    return _rms(o[:t] - lam * o[t:], g) * (1.0 - lam_init)


_NT = (((1,), (1,)), ((), ()))
_NOT_VISIBLE = object()


def _attn_prompt_kernel(q_ref, k_ref, v_ref, bd_ref, bp_ref, lq1, lk1, lq2, lk2, gs_ref, o_ref,
                        kb_ref, vb_ref, *state, t, hd, hps, nt, scale, lam_init):
    first = pl.program_id(2) * nt
    hw = 2 * hd

    @pl.when(first == 0)
    def _():
        kb_ref[...] = k_ref[...].astype(BF16)
        for g in range(hps):
            vb_ref[:, 2 * g * hw:(2 * g + 1) * hw] = v_ref[:, g * hw:(g + 1) * hw].astype(BF16)
            vb_ref[:, (2 * g + 1) * hw:(2 * g + 2) * hw] = jnp.ones((vb_ref.shape[0], hw), BF16)

    q = q_ref[...] * scale
    lane = lax.broadcasted_iota(jnp.int32, (t, hw), 1)
    chains = []
    for u in range(nt):
        for g in range(hps):
            qg = q[u * t:(u + 1) * t, g * hw:(g + 1) * hw]
            chains.append((u, g, jnp.where(lane < hd, qg, 0.0).astype(BF16)))
            chains.append((u, g, jnp.where(lane >= hd, qg, 0.0).astype(BF16)))

    m_scrs, acc_scrs = state[:len(chains)], state[len(chains):]
    for m_scr, acc_scr in zip(m_scrs, acc_scrs):
        m_scr[...] = jnp.full(m_scr.shape, NEG_INF, F32)
        acc_scr[...] = jnp.zeros(acc_scr.shape, F32)

    sb = bd_ref.shape[-1]
    nsb = t // sb

    def add_near_bias(s, kind, g):
        if kind == "prev":
            top = s[:sb]
            top = jnp.concatenate([top[:, :t - sb], top[:, t - sb:] + bp_ref[g]], axis=1)
            return jnp.concatenate([top, s[sb:]], axis=0)
        rows = []
        for i in range(nsb):
            si = s[i * sb:(i + 1) * sb]
            parts = [si[:, :(i - 1) * sb]] if i >= 2 else []
            if i >= 1:
                parts.append(si[:, (i - 1) * sb:i * sb] + bp_ref[g])
            parts.append(si[:, i * sb:(i + 1) * sb] + bd_ref[g])
            if i < nsb - 1:
                parts.append(jnp.full((sb, (nsb - 1 - i) * sb), NEG_INF, F32))
            rows.append(jnp.concatenate(parts, axis=1))
        return jnp.concatenate(rows, axis=0)

    def chunk(j, kind_of_tile):
        start = pl.multiple_of(j * t, t)
        for (u, g, qz), m_scr, acc_scr in zip(chains, m_scrs, acc_scrs):
            kind = kind_of_tile[u]
            if kind is _NOT_VISIBLE:
                continue
            s = lax.dot_general(qz, kb_ref[pl.ds(start, t), g * hw:(g + 1) * hw], _NT, preferred_element_type=F32)
            if kind is not None:
                s = add_near_bias(s, kind, g)
            m_old = m_scr[...]
            m_new = jnp.maximum(m_old, jnp.max(s, axis=1, keepdims=True))
            alpha = jnp.exp2(m_old - m_new)
            p = jnp.exp2(s - jnp.concatenate([m_new] * (t // LANES), axis=1))
            m_scr[...] = m_new
            acc_scr[...] = jnp.concatenate([alpha] * (2 * hw // LANES), axis=1) * acc_scr[...] + jnp.dot(
                p.astype(BF16), vb_ref[pl.ds(start, t), 2 * g * hw:(2 * g + 2) * hw], preferred_element_type=F32)

    def far_chunk(j, carry):
        chunk(j, [None] * nt)
        return carry

    lax.fori_loop(0, jnp.maximum(first - 1, 0), far_chunk, 0)
    for e in range(nt + 1):
        rel = [e - 1 - u for u in range(nt)]
        kinds = [None if r <= -2 else "prev" if r == -1 else "diag" if r == 0 else _NOT_VISIBLE for r in rel]
        if e == 0:
            pl.when(first >= 1)(functools.partial(chunk, first - 1, kinds))
        else:
            chunk(first + e - 1, kinds)

    lam = _lam(lq1, lk1, lq2, lk2, lam_init)
    for u in range(nt):
        for g in range(hps):
            accs = [acc_scrs[(u * hps + g) * 2 + c] for c in range(2)]
            o = jnp.concatenate([a[:, :hw] / a[:, hw:] for a in accs], axis=0)
            o_ref[u * t:(u + 1) * t, g * hw:(g + 1) * hw] = _diff_finish(o, t, lam, gs_ref[...], lam_init)


def _attn_prompt(proj, bias_diag, bias_prev, lams, gs, *, n_seq, seq_len, n_heads, hd, t, hps, nt, lam_init):
    n, vw = n_seq * seq_len, n_heads * 2 * hd
    nq = seq_len // (t * nt)
    bw = hps * 2 * hd
    ng = n_heads // hps
    assert t % bias_diag.shape[-1] == 0
    kern = functools.partial(_attn_prompt_kernel, t=t, hd=hd, hps=hps, nt=nt, scale=hd ** -0.5 * LOG2E,
                             lam_init=lam_init)
    vec = lambda a: pl.BlockSpec(a.shape, lambda h, b, i: (0, 0))
    return pl.pallas_call(
        kern,
        grid=(ng, n_seq, nq),
        in_specs=[pl.BlockSpec((nt * t, bw), lambda h, b, i: (b * nq + i, h)),
                  pl.BlockSpec((seq_len, bw), lambda h, b, i: (b, ng + h)),
                  pl.BlockSpec((seq_len, bw), lambda h, b, i: (b, 2 * ng + h)),
                  pl.BlockSpec((hps,) + bias_diag.shape[1:], lambda h, b, i: (h, 0, 0)),
                  pl.BlockSpec((hps,) + bias_prev.shape[1:], lambda h, b, i: (h, 0, 0)),
                  *[vec(a) for a in lams], vec(gs)],
        out_specs=pl.BlockSpec((nt * t, bw), lambda h, b, i: (b * nq + i, h)),
        out_shape=jax.ShapeDtypeStruct((n, vw), F32),
        scratch_shapes=[pltpu.VMEM((seq_len, bw), BF16), pltpu.VMEM((seq_len, 2 * bw), BF16),
                        *[pltpu.VMEM((t, LANES), F32)] * (2 * hps * nt),
                        *[pltpu.VMEM((t, 4 * hd), F32)] * (2 * hps * nt)],
        compiler_params=_cparams("parallel", "parallel", "arbitrary"),
        name="attn_prompt",
    )(proj, proj, proj, bias_diag, bias_prev, *lams, gs)


def _attn_sample_kernel(pt_ref, q_ref, kn_ref, vn_ref, ck_hbm, cv_hbm, mask_ref, mlast_ref, bn_ref,
                        lq1, lk1, lq2, lk2, gs_ref, o_ref, kbuf, vbuf, sems, qr_scr, m_scr, l_scr, acc_scr,
                        *, pages, group, base, n_heads, hd, t, scale, lam_init):
    c = pl.program_id(1)
    n_chunks = pl.num_programs(1)
    step = pl.program_id(0) * n_chunks + c
    total = pl.num_programs(0) * n_chunks
    depth = kbuf.shape[0]
    hw = 2 * hd

    def page_copies(st, slot):
        sb, sc = st // n_chunks, st % n_chunks
        out = []
        for i in range(pages):
            page = base + pt_ref[sb, sc * pages + i]
            out.append(pltpu.make_async_copy(ck_hbm.at[page], kbuf.at[slot, i], sems.at[slot, 0]))
            out.append(pltpu.make_async_copy(cv_hbm.at[page], vbuf.at[slot, i], sems.at[slot, 1]))
        return out

    @pl.when(step == 0)
    def _():
        for ahead in range(depth - 1):
            for cp in page_copies(ahead, ahead):
                cp.start()

    ahead = step + depth - 1

    @pl.when(ahead < total)
    def _():
        for cp in page_copies(ahead, ahead % depth):
            cp.start()

    slot = step % depth
    for cp in page_copies(step, slot):
        cp.wait()
    kp = [kbuf.at[slot, i] for i in range(pages)]
    vp = [vbuf.at[slot, i] for i in range(pages)]

    @pl.when(c == 0)
    def _():
        q = q_ref[...] * scale
        qr_scr[...] = jnp.concatenate(
            [_two_map_queries(q[:, h * hw:(h + 1) * hw], hd) for h in range(n_heads)], axis=0)
        m_scr[...] = jnp.full(m_scr.shape, NEG_INF, F32)
        l_scr[...] = jnp.zeros(l_scr.shape, F32)
        acc_scr[...] = jnp.zeros(acc_scr.shape, F32)

    qrows = qr_scr[...]
    carry = (m_scr[...], l_scr[...], acc_scr[...])
    for i0 in range(0, pages, group):
        m_old, l, acc = carry
        scores = []
        m_new = m_old
        for i in range(i0, i0 + group):
            table = mlast_ref[0] if i == pages - 1 else mask_ref[...]
            s = lax.dot_general(qrows, kp[i][...].astype(BF16), _NT, preferred_element_type=F32) + table
            m_new = jnp.maximum(m_new, jnp.max(s, axis=1, keepdims=True))
            scores.append(s)
        alpha = jnp.exp2(m_old - m_new)
        l = alpha * l
        acc = alpha * acc
        for i, s in zip(range(i0, i0 + group), scores):
            p = jnp.exp2(s - m_new)
            l = l + jnp.sum(p, axis=1, keepdims=True)
            acc = acc + jnp.dot(p.astype(BF16), vp[i][...].astype(BF16), preferred_element_type=F32)
        carry = (m_new, l, acc)
    m_scr[...], l_scr[...], acc_scr[...] = carry

    @pl.when(c == pl.num_programs(1) - 1)
    def _():
        lam = _lam(lq1, lk1, lq2, lk2, lam_init)
        pad = jnp.zeros((qrows.shape[0] - kn_ref.shape[0], hw), F32)
        kn = jnp.concatenate([kn_ref[...], pad], axis=0).astype(BF16)
        vn = jnp.concatenate([vn_ref[...], pad], axis=0).astype(BF16)
        s = lax.dot_general(qrows, kn, _NT, preferred_element_type=F32) + bn_ref[...]
        _, l, acc = _online_softmax_step(s, *carry, vn)
        o = acc / l
        for h in range(n_heads):
            o_ref[:, h * hw:(h + 1) * hw] = _diff_finish(o[h * 2 * t:(h + 1) * 2 * t], t, lam, gs_ref[...], lam_init)


def _attn_sample(page_table, proj, k_new, v_new, cache_k, cache_v, base, mask, mask_last, bias_new, lams, gs,
                 *, n_seq, t, n_heads, hd, pages, group, ring, lam_init):
    hw = 2 * hd
    vw = n_heads * hw
    nrow = n_heads * 2 * t
    n_pages = page_table.shape[1]
    assert n_pages % pages == 0 and nrow == LANES
    n_chunks = n_pages // pages
    page_rows = cache_k.shape[1]
    assert pages % group == 0 and n_seq * n_chunks >= ring - 1
    kern = functools.partial(_attn_sample_kernel, pages=pages, group=group, base=base, n_heads=n_heads, hd=hd,
                             t=t, scale=hd ** -0.5 * LOG2E, lam_init=lam_init)
    vec = lambda a: pl.BlockSpec(a.shape, lambda b, c, pt: (0,) * a.ndim)
    grid_spec = pltpu.PrefetchScalarGridSpec(
        num_scalar_prefetch=1,
        grid=(n_seq, n_chunks),
        in_specs=[pl.BlockSpec((t, vw), lambda b, c, pt: (b, 0)),
                  pl.BlockSpec((None, t * n_heads, hw), lambda b, c, pt: (b, 0, 0)),
                  pl.BlockSpec((None, t * n_heads, hw), lambda b, c, pt: (b, 0, 0)),
                  pl.BlockSpec(memory_space=pl.ANY), pl.BlockSpec(memory_space=pl.ANY),
                  vec(mask),
                  pl.BlockSpec((1,) + mask_last.shape[1:],
                               lambda b, c, pt: (jnp.where(c == n_chunks - 1, 1, 0), 0, 0)),
                  vec(bias_new), *[vec(a) for a in lams], vec(gs)],
        out_specs=pl.BlockSpec((t, vw), lambda b, c, pt: (b, 0)),
        scratch_shapes=[pltpu.VMEM((ring, pages, page_rows, hw), cache_k.dtype),
                        pltpu.VMEM((ring, pages, page_rows, hw), cache_v.dtype),
                        pltpu.SemaphoreType.DMA((ring, 2)),
                        pltpu.VMEM((nrow, hw), BF16),
                        pltpu.VMEM((nrow, 1), F32),
                        pltpu.VMEM((nrow, 1), F32),
                        pltpu.VMEM((nrow, hw), F32)],
    )
    return pl.pallas_call(
        kern,
        grid_spec=grid_spec,
        out_shape=jax.ShapeDtypeStruct((n_seq * t, vw), F32),
        compiler_params=_cparams("arbitrary", "arbitrary"),
        name="attn_sample",
    )(page_table, proj, k_new, v_new, cache_k, cache_v, mask, mask_last, bias_new, *lams, gs)


def _post_kernel(x_ref, o_ref, r_ref, ga_ref, gr_ref, p_ref, bm_ref, wo_ref, gf_ref, wfi_ref, wfo_ref,
                 gp_ref, wpg_ref, wpp_ref, gfin_ref, y_ref, *, d_ff, final_norm):
    bm = bm_ref[...]
    mix = (jax.nn.sigmoid(ga_ref[...] + bm[0:1]) * o_ref[...]
           + jax.nn.sigmoid(gr_ref[...] + bm[1:2]) * r_ref[...])
    h = x_ref[...] + jnp.dot(mix.astype(BF16), wo_ref[...], preferred_element_type=F32)
    hn = _rms(h, gf_ref[...]).astype(BF16)
    gate = jnp.dot(hn, wfi_ref[:, :d_ff], preferred_element_type=F32)
    up = jnp.dot(hn, wfi_ref[:, d_ff:], preferred_element_type=F32)
    act = (jax.nn.silu(gate) * up).astype(BF16)
    h = h + jnp.dot(act, wfo_ref[...], preferred_element_type=F32)
    hp = _rms(h, gp_ref[...]).astype(BF16)
    gate = jax.nn.sigmoid(jnp.dot(hp, wpg_ref[...], preferred_element_type=F32))
    h = h + gate * jnp.dot(p_ref[...].astype(BF16), wpp_ref[...], preferred_element_type=F32)
    y_ref[...] = _rms(h, gfin_ref[...]) if final_norm else h


def _post(x, o, r, proj, col_ga, col_gr, p, bm, wo, gf, wfi, wfo, gp, wpg, wpp, gfin, *, tm, final_norm):
    n, d = x.shape
    kern = functools.partial(_post_kernel, d_ff=wfo.shape[0], final_norm=final_norm)
    tok = lambda col: pl.BlockSpec((tm, d), lambda i: (i, col))
    return pl.pallas_call(
        kern,
        grid=(n // tm,),
        in_specs=[tok(0), tok(0), tok(0), tok(col_ga), tok(col_gr),
                  pl.BlockSpec((tm, p.shape[1]), lambda i: (i, 0)),
                  *[_const_spec(a.shape) for a in (bm, wo, gf, wfi, wfo, gp, wpg, wpp, gfin)]],
        out_specs=pl.BlockSpec((tm, d), lambda i: (i, 0)),
        out_shape=jax.ShapeDtypeStruct((n, d), F32),
        compiler_params=_cparams("parallel"),
        name="post",
    )(x, o, r, proj, proj, p, bm, wo, gf, wfi, wfo, gp, wpg, wpp, gfin)


def _t5_bucket(n, n_buckets):
    max_exact = n_buckets // 2
    nf = jnp.maximum(n, 1).astype(F32)
    large = max_exact + (jnp.log(nf / max_exact) / math.log(MAX_DISTANCE / max_exact)
                         * (n_buckets - max_exact)).astype(jnp.int32)
    large = jnp.minimum(large, n_buckets - 1)
    return jnp.where(n < max_exact, n, large)


def _far_distance(n_buckets):
    n = np.arange(1, 4 * MAX_DISTANCE, dtype=np.int64)
    max_exact = n_buckets // 2
    large = max_exact + (np.log(n.astype(np.float32) / max_exact) / math.log(MAX_DISTANCE / max_exact)
                         * (n_buckets - max_exact)).astype(np.int64)
    bucket = np.where(n < max_exact, n, np.minimum(large, n_buckets - 1))
    not_last = np.nonzero(bucket != n_buckets - 1)[0]
    return int(n[not_last[-1]]) + 2


def _shifted_bias(rel_bias, dist):
    nb, nh = rel_bias.shape
    rel = rel_bias.astype(F32) - rel_bias[nb - 1].astype(F32)
    bucket = _t5_bucket(jnp.maximum(dist, 0), nb)
    b = jnp.zeros((nh,) + dist.shape, F32)
    for k in range(nb - 1):
        b = jnp.where(bucket == k, rel[k].reshape((nh,) + (1,) * dist.ndim), b)
    return jnp.where(dist >= 0, b * LOG2E, NEG_INF)


def _gate_weights(w_gx, w_ga):
    nb, bw, _ = w_gx.shape
    per = LANES // bw

    def blockdiag(wb):
        wb = wb.reshape(nb // per, per, bw, bw)
        eye = jnp.eye(per, dtype=wb.dtype)
        return jnp.einsum('gpij,pq->gpiqj', wb, eye).reshape(nb // per, LANES, LANES)

    return jnp.concatenate([blockdiag(w_gx), blockdiag(w_ga)], axis=-1).astype(BF16)


def kernel(x_prompt, x_sample, p_prompt, p_sample, cache_k, cache_v, state_conv, state_rnn, page_table,
           rel_bias, g_mix, w_in, conv_w, conv_b, w_gx, b_gx, w_ga, b_ga, a_param,
           lam_q1, lam_k1, lam_q2, lam_k2, g_subln, b_merge, w_out,
           g_ffn, w_ffn_in, w_ffn_out, g_ple, w_ple_gate, w_ple_proj, g_final):
    bp, sp, d = x_prompt.shape
    bs, ts, _ = x_sample.shape
    depth, n_pool, page_size, n_heads, hw = cache_k.shape
    hd = hw // 2
    n_pages = page_table.shape[1]
    past = n_pages * page_size
    lru_w = conv_w.shape[2]
    kw = conv_w.shape[1]
    tl = TILES
    assert d == n_heads * hw == lru_w and w_in.shape[2] == 7 * d and hw == LANES
    assert ts == SUBLANES and kw - 1 <= min(ts, SUBLANES)

    far = _far_distance(rel_bias.shape[0])
    assert LANES + 1 >= far and page_size + 1 >= far

    r = jnp.arange(LANES, dtype=jnp.int32)
    diag = r[:, None] - r[None, :]
    bias_diag = _shifted_bias(rel_bias, diag)
    bias_prev = _shifted_bias(rel_bias, diag + LANES)
    tq = jnp.arange(ts, dtype=jnp.int32)
    kk = jnp.arange(page_size, dtype=jnp.int32)
    same_head = jnp.eye(n_heads, dtype=bool)[:, None, None, None, :]

    def head_table(b):
        full = jnp.where(same_head, b[:, None, :, :, None], NEG_INF)
        full = jnp.broadcast_to(full, (n_heads, 2, ts, b.shape[2], n_heads))
        return full.reshape(n_heads * 2 * ts, b.shape[2] * n_heads)

    mask = head_table(jnp.zeros((n_heads, ts, page_size), F32))
    mask_last = jnp.stack([mask, head_table(_shifted_bias(rel_bias, page_size + tq[:, None] - kk[None, :]))])
    bias_new = head_table(_shifted_bias(rel_bias, tq[:, None] - tq[None, :]))
    bias_new = jnp.pad(bias_new, ((0, 0), (0, LANES - ts * n_heads)), constant_values=NEG_INF)

    ck = cache_k.reshape(depth * n_pool, page_size * n_heads, hw)
    cv = cache_v.reshape(depth * n_pool, page_size * n_heads, hw)

    hp = x_prompt.reshape(bp * sp, d)
    hs = x_sample.reshape(bs * ts, d)
    row = lambda a: a.reshape(1, -1).astype(F32)
    outs = {k: [] for k in ("kp", "vp", "cp", "rp", "ks", "vs", "cs", "rs")}
    for l in range(depth):
        lam_init = 0.8 - 0.6 * math.exp(-0.3 * l)
        w_in_b = w_in[l].astype(BF16)
        wg = _gate_weights(w_gx[l], w_ga[l])
        lams = [row(a[l]) for a in (lam_q1, lam_k1, lam_q2, lam_k2)]
        gs = row(g_subln[l])
        rglru_w = (conv_w[l], row(conv_b[l]), wg, row(b_gx[l]), row(b_ga[l]), row(a_param[l]))
        post_w = (b_merge[l], w_out[l].astype(BF16), row(g_ffn[l]), w_ffn_in[l].astype(BF16),
                  w_ffn_out[l].astype(BF16), row(g_ple[l]), w_ple_gate[l].astype(BF16),
                  w_ple_proj[l].astype(BF16), row(g_final))
        last = l == depth - 1

        proj, k_rows, v_rows = _inproj(hp, row(g_mix[l]), w_in_b, tm=tl.inproj_rows, tn=d, k_col=1, v_col=2,
                                       n_heads=n_heads)
        r_p, hl_p = _rglru(proj, 3, 4, jnp.zeros((bp, SUBLANES, lru_w), F32), jnp.zeros((bp, 1, lru_w), F32),
                           *rglru_w, n_seq=bp, seq_len=sp, rows=tl.rglru_rows, reset_first=True)
        o_p = _attn_prompt(proj, bias_diag, bias_prev, lams, gs, n_seq=bp, seq_len=sp, n_heads=n_heads,
                           hd=hd, t=tl.attn_t, hps=tl.attn_heads, nt=tl.attn_tiles, lam_init=lam_init)
        hp = _post(hp, o_p, r_p, proj, 5, 6, p_prompt[l].reshape(bp * sp, -1), *post_w,
                   tm=tl.post_rows, final_norm=last)
        proj3 = proj.reshape(bp, sp, 7 * d)
        outs["kp"].append(k_rows.reshape(bp, sp, n_heads, hw))
        outs["vp"].append(v_rows.reshape(bp, sp, n_heads, hw))
        outs["cp"].append(proj3[:, sp - (kw - 1):, 3 * d:4 * d])
        outs["rp"].append(hl_p[:, SUBLANES - 1])

        proj, k_rows, v_rows = _inproj(hs, row(g_mix[l]), w_in_b, tm=bs * ts, tn=d, k_col=1, v_col=2,
                                       n_heads=n_heads)
        tail0 = jnp.pad(state_conv[l].astype(F32), ((0, 0), (SUBLANES - (kw - 1), 0), (0, 0)))
        r_s, hl_s = _rglru(proj, 3, 4, tail0, state_rnn[l].astype(F32)[:, None, :],
                           *rglru_w, n_seq=bs, seq_len=ts, rows=ts, reset_first=(past == 0))
        proj3 = proj.reshape(bs, ts, 7 * d)
        k_s = k_rows.reshape(bs, ts, n_heads, hw)
        v_s = v_rows.reshape(bs, ts, n_heads, hw)
        o_s = _attn_sample(page_table, proj, k_rows.reshape(bs, ts * n_heads, hw),
                           v_rows.reshape(bs, ts * n_heads, hw),
                           ck, cv, l * n_pool, mask, mask_last, bias_new, lams, gs,
                           n_seq=bs, t=ts, n_heads=n_heads, hd=hd, pages=tl.pages, group=tl.page_group, ring=tl.page_ring,
                           lam_init=lam_init)
        hs = _post(hs, o_s, r_s, proj, 5, 6, p_sample[l].reshape(bs * ts, -1), *post_w,
                   tm=bs * ts, final_norm=last)
        outs["ks"].append(k_s)
        outs["vs"].append(v_s)
        outs["cs"].append(proj3[:, ts - (kw - 1):, 3 * d:4 * d])
        outs["rs"].append(hl_s[:, SUBLANES - 1])

    st = lambda k: jnp.stack(outs[k])
    return (hp.reshape(bp, sp, d), hs.reshape(bs, ts, d),
            st("kp"), st("vp"), st("cp"), st("rp"), st("ks"), st("vs"), st("cs"), st("rs"))
```

```python
import functools
import math
from typing import NamedTuple

import jax
import jax.numpy as jnp
import numpy as np
from jax import lax
from jax.experimental import pallas as pl
from jax.experimental.pallas import tpu as pltpu

F32 = jnp.float32
BF16 = jnp.bfloat16

EPS = 1e-6
NEG_INF = -1e30
MAX_DISTANCE = 128
LRU_C = 8.0
LOG2E = math.log2(math.e)
LANES = 128
SUBLANES = 8
VMEM_LIMIT = 56 * 1024 * 1024


class Tiles(NamedTuple):
    inproj_rows: int = 256
    rglru_rows: int = 256
    attn_t: int = 512
    attn_heads: int = 2
    attn_tiles: int = 2
    pages: int = 8
    page_group: int = 8
    page_ring: int = 4
    post_rows: int = 256


TILES = Tiles()


def _cparams(*sem):
    return pltpu.CompilerParams(dimension_semantics=sem, vmem_limit_bytes=VMEM_LIMIT)


def _rms(x, g):
    return x * lax.rsqrt(jnp.mean(x * x, axis=-1, keepdims=True) + EPS) * g


def _const_spec(shape):
    nd = len(shape)
    return pl.BlockSpec(shape, lambda *_: (0,) * nd, pipeline_mode=pl.Buffered(1))


def _inproj_kernel(x_ref, g_ref, w_ref, o_ref, k_ref, v_ref, *, tn, k_col, v_col, n_heads):
    xn = _rms(x_ref[...], g_ref[...]).astype(BF16)
    tm, hw = xn.shape[0], tn // n_heads
    for j in range(w_ref.shape[1] // tn):
        res = jnp.dot(xn, w_ref[:, j * tn:(j + 1) * tn], preferred_element_type=F32)
        o_ref[:, j * tn:(j + 1) * tn] = res
        for col, dst_ref in ((k_col, k_ref), (v_col, v_ref)):
            if j == col:
                for h in range(n_heads):
                    dst_ref[pl.ds(h, tm, stride=n_heads), :] = res[:, h * hw:(h + 1) * hw]


def _inproj(x, g, w_bf16, tm, tn, k_col, v_col, n_heads):
    n, d = x.shape
    nw = w_bf16.shape[1]
    hw = tn // n_heads
    kern = functools.partial(_inproj_kernel, tn=tn, k_col=k_col, v_col=v_col, n_heads=n_heads)
    return pl.pallas_call(
        kern,
        grid=(n // tm,),
        in_specs=[pl.BlockSpec((tm, d), lambda i: (i, 0)),
                  _const_spec((1, d)), _const_spec((d, nw))],
        out_specs=[pl.BlockSpec((tm, nw), lambda i: (i, 0)),
                   pl.BlockSpec((tm * n_heads, hw), lambda i: (i, 0)),
                   pl.BlockSpec((tm * n_heads, hw), lambda i: (i, 0))],
        out_shape=[jax.ShapeDtypeStruct((n, nw), F32),
                   jax.ShapeDtypeStruct((n * n_heads, hw), F32),
                   jax.ShapeDtypeStruct((n * n_heads, hw), F32)],
        compiler_params=_cparams("parallel"),
        name="inproj",
    )(x, g, w_bf16)


def _rglru_kernel(xr_ref, yr_ref, tail_ref, h0_ref, cw_ref, cb_ref, wg_ref, bgx_ref, bga_ref, ap_ref,
                  r_ref, hlast_ref, win_scr, carry_scr, *, nseq, rows, reset_first):
    s = pl.program_id(1)
    w = xr_ref.shape[-1]

    @pl.when(s == 0)
    def _():
        win_scr[:, :SUBLANES, :] = tail_ref[...]
        carry_scr[...] = h0_ref[...]

    x = xr_ref[...].reshape(nseq, rows, w)
    cw = cw_ref[...]
    kw = cw.shape[0]
    row8 = lax.broadcasted_iota(jnp.int32, (1, SUBLANES, w), 1)
    win_scr[:, SUBLANES:, :] = x
    xc = cb_ref[...] + x * cw[kw - 1:kw]
    for k in range(1, kw):
        xc = xc + win_scr[:, SUBLANES - k:SUBLANES - k + rows, :] * cw[kw - 1 - k:kw - k]
    win_scr[:, :SUBLANES, :] = x[:, rows - SUBLANES:, :]

    xc2 = xc.reshape(nseq * rows, w)
    xcb = xc2.astype(BF16)
    gx_pre, ga_pre = [], []
    for g in range(w // LANES):
        pre = jnp.dot(xcb[:, g * LANES:(g + 1) * LANES], wg_ref[g], preferred_element_type=F32)
        gx_pre.append(pre[:, :LANES])
        ga_pre.append(pre[:, LANES:])
    gx = jax.nn.sigmoid(jnp.concatenate(gx_pre, axis=1) + bgx_ref[...])
    ga = jax.nn.sigmoid(jnp.concatenate(ga_pre, axis=1) + bga_ref[...])

    z = -ap_ref[...]
    softplus = jnp.maximum(z, 0.0) + jnp.log1p(jnp.exp(-jnp.abs(z)))
    log_a = -LRU_C * ga * softplus
    a = jnp.exp(log_a)
    mult = jnp.sqrt(-jnp.tanh(log_a) * (a * a + 1.0))
    if reset_first:
        first = row8 == jnp.where(s == 0, 0, -1)

        def reset(v, value):
            v = v.reshape(nseq, rows, w)
            head = jnp.where(first, value, v[:, :SUBLANES, :])
            v = jnp.concatenate([head, v[:, SUBLANES:, :]], axis=1) if rows > SUBLANES else head
            return v.reshape(nseq * rows, w)

        a = reset(a, 0.0)
        mult = reset(mult, 1.0)
    b = xc2 * gx * mult

    ngrp = nseq * rows // SUBLANES
    av = a.reshape(ngrp, SUBLANES, w)
    bv = b.reshape(ngrp, SUBLANES, w)
    d = 1
    while d < SUBLANES:
        keep = row8 >= d
        a_prev = jnp.where(keep, pltpu.roll(av, d, axis=1), 1.0)
        b_prev = jnp.where(keep, pltpu.roll(bv, d, axis=1), 0.0)
        bv = av * b_prev + bv
        av = av * a_prev
        d *= 2
    av = av.reshape(nseq, rows, w)
    bv = bv.reshape(nseq, rows, w)
    carry = carry_scr[...]
    hs = []
    for g in range(rows // SUBLANES):
        sl = slice(g * SUBLANES, (g + 1) * SUBLANES)
        hg = av[:, sl, :] * carry + bv[:, sl, :]
        carry = hg[:, SUBLANES - 1:, :]
        hs.append(hg)
    carry_scr[...] = carry
    hlast_ref[...] = hs[-1]
    h = jnp.concatenate(hs, axis=1) if len(hs) > 1 else hs[0]
    r_ref[...] = h.reshape(nseq * rows, w) * jax.nn.gelu(yr_ref[...])


def _rglru(proj, col_x, col_y, tail0, h0, cw, cb, wg, bgx, bga, ap, *, n_seq, seq_len, rows, reset_first):
    w = cw.shape[1]
    n = n_seq * seq_len
    if seq_len >= rows:
        assert seq_len % rows == 0
        nseq_t, steps, grid0 = 1, seq_len // rows, n_seq
    else:
        assert seq_len == rows == SUBLANES
        nseq_t, steps, grid0 = n_seq, 1, 1
    tm = nseq_t * rows
    kern = functools.partial(_rglru_kernel, nseq=nseq_t, rows=rows, reset_first=reset_first)
    small = lambda shape: pl.BlockSpec(shape, lambda b, s: (0,) * len(shape))
    return pl.pallas_call(
        kern,
        grid=(grid0, steps),
        in_specs=[pl.BlockSpec((tm, w), lambda b, s: (b * steps + s, col_x)),
                  pl.BlockSpec((tm, w), lambda b, s: (b * steps + s, col_y)),
                  pl.BlockSpec((nseq_t, SUBLANES, w), lambda b, s: (b, 0, 0)),
                  pl.BlockSpec((nseq_t, 1, w), lambda b, s: (b, 0, 0)),
                  small(cw.shape), small(cb.shape), small(wg.shape),
                  small(bgx.shape), small(bga.shape), small(ap.shape)],
        out_specs=[pl.BlockSpec((tm, w), lambda b, s: (b * steps + s, 0)),
                   pl.BlockSpec((nseq_t, SUBLANES, w), lambda b, s: (b, 0, 0))],
        out_shape=[jax.ShapeDtypeStruct((n, w), F32),
                   jax.ShapeDtypeStruct((n_seq, SUBLANES, w), F32)],
        scratch_shapes=[pltpu.VMEM((nseq_t, SUBLANES + rows, w), F32),
                        pltpu.VMEM((nseq_t, 1, w), F32)],
        compiler_params=_cparams("parallel", "arbitrary"),
        name="rglru",
    )(proj, proj, tail0, h0, cw, cb, wg, bgx, bga, ap)


def _lam(lq1, lk1, lq2, lk2, lam_init):
    return (jnp.exp(jnp.sum(lq1[...] * lk1[...], axis=-1, keepdims=True))
            - jnp.exp(jnp.sum(lq2[...] * lk2[...], axis=-1, keepdims=True)) + lam_init)


def _two_map_queries(q, hd):
    lane = lax.broadcasted_iota(jnp.int32, q.shape, 1)
    return jnp.concatenate([jnp.where(lane < hd, q, 0.0), jnp.where(lane >= hd, q, 0.0)], axis=0).astype(BF16)


def _softmax_numerators(s, m):
    m_new = jnp.maximum(m, jnp.max(s, axis=1, keepdims=True))
    return m_new, jnp.exp2(m - m_new), jnp.exp2(s - m_new)


def _online_softmax_step(s, m, l, acc, v):
    m_new, alpha, p = _softmax_numerators(s, m)
    l = alpha * l + jnp.sum(p, axis=1, keepdims=True)
    acc = alpha * acc + jnp.dot(p.astype(BF16), v, preferred_element_type=F32)
    return m_new, l, acc


def _diff_finish(o, t, lam, g, lam_init):
    return _rms(o[:t] - lam * o[t:], g) * (1.0 - lam_init)


_NT = (((1,), (1,)), ((), ()))
_NOT_VISIBLE = object()


def _attn_prompt_kernel(q_ref, k_ref, v_ref, bd_ref, bp_ref, lq1, lk1, lq2, lk2, gs_ref, o_ref,
                        kb_ref, vb_ref, *state, t, hd, hps, nt, scale, lam_init):
    first = pl.program_id(2) * nt
    hw = 2 * hd

    @pl.when(first == 0)
    def _():
        kb_ref[...] = k_ref[...].astype(BF16)
        for g in range(hps):
            vb_ref[:, 2 * g * hw:(2 * g + 1) * hw] = v_ref[:, g * hw:(g + 1) * hw].astype(BF16)
            vb_ref[:, (2 * g + 1) * hw:(2 * g + 2) * hw] = jnp.ones((vb_ref.shape[0], hw), BF16)

    q = q_ref[...] * scale
    lane = lax.broadcasted_iota(jnp.int32, (t, hw), 1)
    chains = []
    for u in range(nt):
        for g in range(hps):
            qg = q[u * t:(u + 1) * t, g * hw:(g + 1) * hw]
            chains.append((u, g, jnp.where(lane < hd, qg, 0.0).astype(BF16)))
            chains.append((u, g, jnp.where(lane >= hd, qg, 0.0).astype(BF16)))

    m_scrs, acc_scrs = state[:len(chains)], state[len(chains):]
    for m_scr, acc_scr in zip(m_scrs, acc_scrs):
        m_scr[...] = jnp.full(m_scr.shape, NEG_INF, F32)
        acc_scr[...] = jnp.zeros(acc_scr.shape, F32)

    sb = bd_ref.shape[-1]
    nsb = t // sb

    def add_near_bias(s, kind, g):
        if kind == "prev":
            top = s[:sb]
            top = jnp.concatenate([top[:, :t - sb], top[:, t - sb:] + bp_ref[g]], axis=1)
            return jnp.concatenate([top, s[sb:]], axis=0)
        rows = []
        for i in range(nsb):
            si = s[i * sb:(i + 1) * sb]
            parts = [si[:, :(i - 1) * sb]] if i >= 2 else []
            if i >= 1:
                parts.append(si[:, (i - 1) * sb:i * sb] + bp_ref[g])
            parts.append(si[:, i * sb:(i + 1) * sb] + bd_ref[g])
            if i < nsb - 1:
                parts.append(jnp.full((sb, (nsb - 1 - i) * sb), NEG_INF, F32))
            rows.append(jnp.concatenate(parts, axis=1))
        return jnp.concatenate(rows, axis=0)

    def chunk(j, kind_of_tile):
        start = pl.multiple_of(j * t, t)
        for (u, g, qz), m_scr, acc_scr in zip(chains, m_scrs, acc_scrs):
            kind = kind_of_tile[u]
            if kind is _NOT_VISIBLE:
                continue
            s = lax.dot_general(qz, kb_ref[pl.ds(start, t), g * hw:(g + 1) * hw], _NT, preferred_element_type=F32)
            if kind is not None:
                s = add_near_bias(s, kind, g)
            m_old = m_scr[...]
            m_new = jnp.maximum(m_old, jnp.max(s, axis=1, keepdims=True))
            alpha = jnp.exp2(m_old - m_new)
            p = jnp.exp2(s - jnp.concatenate([m_new] * (t // LANES), axis=1))
            m_scr[...] = m_new
            acc_scr[...] = jnp.concatenate([alpha] * (2 * hw // LANES), axis=1) * acc_scr[...] + jnp.dot(
                p.astype(BF16), vb_ref[pl.ds(start, t), 2 * g * hw:(2 * g + 2) * hw], preferred_element_type=F32)

    def far_chunk(j, carry):
        chunk(j, [None] * nt)
        return carry

    lax.fori_loop(0, jnp.maximum(first - 1, 0), far_chunk, 0)
    for e in range(nt + 1):
        rel = [e - 1 - u for u in range(nt)]
        kinds = [None if r <= -2 else "prev" if r == -1 else "diag" if r == 0 else _NOT_VISIBLE for r in rel]
        if e == 0:
            pl.when(first >= 1)(functools.partial(chunk, first - 1, kinds))
        else:
            chunk(first + e - 1, kinds)

    lam = _lam(lq1, lk1, lq2, lk2, lam_init)
    for u in range(nt):
        for g in range(hps):
            accs = [acc_scrs[(u * hps + g) * 2 + c] for c in range(2)]
            o = jnp.concatenate([a[:, :hw] / a[:, hw:] for a in accs], axis=0)
            o_ref[u * t:(u + 1) * t, g * hw:(g + 1) * hw] = _diff_finish(o, t, lam, gs_ref[...], lam_init)


def _attn_prompt(proj, bias_diag, bias_prev, lams, gs, *, n_seq, seq_len, n_heads, hd, t, hps, nt, lam_init):
    n, vw = n_seq * seq_len, n_heads * 2 * hd
    nq = seq_len // (t * nt)
    bw = hps * 2 * hd
    ng = n_heads // hps
    assert t % bias_diag.shape[-1] == 0
    kern = functools.partial(_attn_prompt_kernel, t=t, hd=hd, hps=hps, nt=nt, scale=hd ** -0.5 * LOG2E,
                             lam_init=lam_init)
    vec = lambda a: pl.BlockSpec(a.shape, lambda h, b, i: (0, 0))
    return pl.pallas_call(
        kern,
        grid=(ng, n_seq, nq),
        in_specs=[pl.BlockSpec((nt * t, bw), lambda h, b, i: (b * nq + i, h)),
                  pl.BlockSpec((seq_len, bw), lambda h, b, i: (b, ng + h)),
                  pl.BlockSpec((seq_len, bw), lambda h, b, i: (b, 2 * ng + h)),
                  pl.BlockSpec((hps,) + bias_diag.shape[1:], lambda h, b, i: (h, 0, 0)),
                  pl.BlockSpec((hps,) + bias_prev.shape[1:], lambda h, b, i: (h, 0, 0)),
                  *[vec(a) for a in lams], vec(gs)],
        out_specs=pl.BlockSpec((nt * t, bw), lambda h, b, i: (b * nq + i, h)),
        out_shape=jax.ShapeDtypeStruct((n, vw), F32),
        scratch_shapes=[pltpu.VMEM((seq_len, bw), BF16), pltpu.VMEM((seq_len, 2 * bw), BF16),
                        *[pltpu.VMEM((t, LANES), F32)] * (2 * hps * nt),
                        *[pltpu.VMEM((t, 4 * hd), F32)] * (2 * hps * nt)],
        compiler_params=_cparams("parallel", "parallel", "arbitrary"),
        name="attn_prompt",
    )(proj, proj, proj, bias_diag, bias_prev, *lams, gs)


def _attn_sample_kernel(pt_ref, q_ref, kn_ref, vn_ref, ck_hbm, cv_hbm, mask_ref, mlast_ref, bn_ref,
                        lq1, lk1, lq2, lk2, gs_ref, o_ref, kbuf, vbuf, sems, qr_scr, m_scr, l_scr, acc_scr,
                        *, pages, group, base, n_heads, hd, t, scale, lam_init):
    c = pl.program_id(1)
    n_chunks = pl.num_programs(1)
    step = pl.program_id(0) * n_chunks + c
    total = pl.num_programs(0) * n_chunks
    depth = kbuf.shape[0]
    hw = 2 * hd

    def page_copies(st, slot):
        sb, sc = st // n_chunks, st % n_chunks
        out = []
        for i in range(pages):
            page = base + pt_ref[sb, sc * pages + i]
            out.append(pltpu.make_async_copy(ck_hbm.at[page], kbuf.at[slot, i], sems.at[slot, 0]))
            out.append(pltpu.make_async_copy(cv_hbm.at[page], vbuf.at[slot, i], sems.at[slot, 1]))
        return out

    @pl.when(step == 0)
    def _():
        for ahead in range(depth - 1):
            for cp in page_copies(ahead, ahead):
                cp.start()

    ahead = step + depth - 1

    @pl.when(ahead < total)
    def _():
        for cp in page_copies(ahead, ahead % depth):
            cp.start()

    slot = step % depth
    for cp in page_copies(step, slot):
        cp.wait()
    kp = [kbuf.at[slot, i] for i in range(pages)]
    vp = [vbuf.at[slot, i] for i in range(pages)]

    @pl.when(c == 0)
    def _():
        q = q_ref[...] * scale
        qr_scr[...] = jnp.concatenate(
            [_two_map_queries(q[:, h * hw:(h + 1) * hw], hd) for h in range(n_heads)], axis=0)
        m_scr[...] = jnp.full(m_scr.shape, NEG_INF, F32)
        l_scr[...] = jnp.zeros(l_scr.shape, F32)
        acc_scr[...] = jnp.zeros(acc_scr.shape, F32)

    qrows = qr_scr[...]
    carry = (m_scr[...], l_scr[...], acc_scr[...])
    for i0 in range(0, pages, group):
        m_old, l, acc = carry
        scores = []
        m_new = m_old
        for i in range(i0, i0 + group):
            table = mlast_ref[0] if i == pages - 1 else mask_ref[...]
            s = lax.dot_general(qrows, kp[i][...].astype(BF16), _NT, preferred_element_type=F32) + table
            m_new = jnp.maximum(m_new, jnp.max(s, axis=1, keepdims=True))
            scores.append(s)
        alpha = jnp.exp2(m_old - m_new)
        l = alpha * l
        acc = alpha * acc
        for i, s in zip(range(i0, i0 + group), scores):
            p = jnp.exp2(s - m_new)
            l = l + jnp.sum(p, axis=1, keepdims=True)
            acc = acc + jnp.dot(p.astype(BF16), vp[i][...].astype(BF16), preferred_element_type=F32)
        carry = (m_new, l, acc)
    m_scr[...], l_scr[...], acc_scr[...] = carry

    @pl.when(c == pl.num_programs(1) - 1)
    def _():
        lam = _lam(lq1, lk1, lq2, lk2, lam_init)
        pad = jnp.zeros((qrows.shape[0] - kn_ref.shape[0], hw), F32)
        kn = jnp.concatenate([kn_ref[...], pad], axis=0).astype(BF16)
        vn = jnp.concatenate([vn_ref[...], pad], axis=0).astype(BF16)
        s = lax.dot_general(qrows, kn, _NT, preferred_element_type=F32) + bn_ref[...]
        _, l, acc = _online_softmax_step(s, *carry, vn)
        o = acc / l
        for h in range(n_heads):
            o_ref[:, h * hw:(h + 1) * hw] = _diff_finish(o[h * 2 * t:(h + 1) * 2 * t], t, lam, gs_ref[...], lam_init)


def _attn_sample(page_table, proj, k_new, v_new, cache_k, cache_v, base, mask, mask_last, bias_new, lams, gs,
                 *, n_seq, t, n_heads, hd, pages, group, ring, lam_init):
    hw = 2 * hd
    vw = n_heads * hw
    nrow = n_heads * 2 * t
    n_pages = page_table.shape[1]
    assert n_pages % pages == 0 and nrow == LANES
    n_chunks = n_pages // pages
    page_rows = cache_k.shape[1]
    assert pages % group == 0 and n_seq * n_chunks >= ring - 1
    kern = functools.partial(_attn_sample_kernel, pages=pages, group=group, base=base, n_heads=n_heads, hd=hd,
                             t=t, scale=hd ** -0.5 * LOG2E, lam_init=lam_init)
    vec = lambda a: pl.BlockSpec(a.shape, lambda b, c, pt: (0,) * a.ndim)
    grid_spec = pltpu.PrefetchScalarGridSpec(
        num_scalar_prefetch=1,
        grid=(n_seq, n_chunks),
        in_specs=[pl.BlockSpec((t, vw), lambda b, c, pt: (b, 0)),
                  pl.BlockSpec((None, t * n_heads, hw), lambda b, c, pt: (b, 0, 0)),
                  pl.BlockSpec((None, t * n_heads, hw), lambda b, c, pt: (b, 0, 0)),
                  pl.BlockSpec(memory_space=pl.ANY), pl.BlockSpec(memory_space=pl.ANY),
                  vec(mask),
                  pl.BlockSpec((1,) + mask_last.shape[1:],
                               lambda b, c, pt: (jnp.where(c == n_chunks - 1, 1, 0), 0, 0)),
                  vec(bias_new), *[vec(a) for a in lams], vec(gs)],
        out_specs=pl.BlockSpec((t, vw), lambda b, c, pt: (b, 0)),
        scratch_shapes=[pltpu.VMEM((ring, pages, page_rows, hw), cache_k.dtype),
                        pltpu.VMEM((ring, pages, page_rows, hw), cache_v.dtype),
                        pltpu.SemaphoreType.DMA((ring, 2)),
                        pltpu.VMEM((nrow, hw), BF16),
                        pltpu.VMEM((nrow, 1), F32),
                        pltpu.VMEM((nrow, 1), F32),
                        pltpu.VMEM((nrow, hw), F32)],
    )
    return pl.pallas_call(
        kern,
        grid_spec=grid_spec,
        out_shape=jax.ShapeDtypeStruct((n_seq * t, vw), F32),
        compiler_params=_cparams("arbitrary", "arbitrary"),
        name="attn_sample",
    )(page_table, proj, k_new, v_new, cache_k, cache_v, mask, mask_last, bias_new, *lams, gs)


def _post_kernel(x_ref, o_ref, r_ref, ga_ref, gr_ref, p_ref, bm_ref, wo_ref, gf_ref, wfi_ref, wfo_ref,
                 gp_ref, wpg_ref, wpp_ref, gfin_ref, y_ref, *, d_ff, final_norm):
    bm = bm_ref[...]
    mix = (jax.nn.sigmoid(ga_ref[...] + bm[0:1]) * o_ref[...]
           + jax.nn.sigmoid(gr_ref[...] + bm[1:2]) * r_ref[...])
    h = x_ref[...] + jnp.dot(mix.astype(BF16), wo_ref[...], preferred_element_type=F32)
    hn = _rms(h, gf_ref[...]).astype(BF16)
    gate = jnp.dot(hn, wfi_ref[:, :d_ff], preferred_element_type=F32)
    up = jnp.dot(hn, wfi_ref[:, d_ff:], preferred_element_type=F32)
    act = (jax.nn.silu(gate) * up).astype(BF16)
    h = h + jnp.dot(act, wfo_ref[...], preferred_element_type=F32)
    hp = _rms(h, gp_ref[...]).astype(BF16)
    gate = jax.nn.sigmoid(jnp.dot(hp, wpg_ref[...], preferred_element_type=F32))
    h = h + gate * jnp.dot(p_ref[...].astype(BF16), wpp_ref[...], preferred_element_type=F32)
    y_ref[...] = _rms(h, gfin_ref[...]) if final_norm else h


def _post(x, o, r, proj, col_ga, col_gr, p, bm, wo, gf, wfi, wfo, gp, wpg, wpp, gfin, *, tm, final_norm):
    n, d = x.shape
    kern = functools.partial(_post_kernel, d_ff=wfo.shape[0], final_norm=final_norm)
    tok = lambda col: pl.BlockSpec((tm, d), lambda i: (i, col))
    return pl.pallas_call(
        kern,
        grid=(n // tm,),
        in_specs=[tok(0), tok(0), tok(0), tok(col_ga), tok(col_gr),
                  pl.BlockSpec((tm, p.shape[1]), lambda i: (i, 0)),
                  *[_const_spec(a.shape) for a in (bm, wo, gf, wfi, wfo, gp, wpg, wpp, gfin)]],
        out_specs=pl.BlockSpec((tm, d), lambda i: (i, 0)),
        out_shape=jax.ShapeDtypeStruct((n, d), F32),
        compiler_params=_cparams("parallel"),
        name="post",
    )(x, o, r, proj, proj, p, bm, wo, gf, wfi, wfo, gp, wpg, wpp, gfin)


def _t5_bucket(n, n_buckets):
    max_exact = n_buckets // 2
    nf = jnp.maximum(n, 1).astype(F32)
    large = max_exact + (jnp.log(nf / max_exact) / math.log(MAX_DISTANCE / max_exact)
                         * (n_buckets - max_exact)).astype(jnp.int32)
    large = jnp.minimum(large, n_buckets - 1)
    return jnp.where(n < max_exact, n, large)


def _far_distance(n_buckets):
    n = np.arange(1, 4 * MAX_DISTANCE, dtype=np.int64)
    max_exact = n_buckets // 2
    large = max_exact + (np.log(n.astype(np.float32) / max_exact) / math.log(MAX_DISTANCE / max_exact)
                         * (n_buckets - max_exact)).astype(np.int64)
    bucket = np.where(n < max_exact, n, np.minimum(large, n_buckets - 1))
    not_last = np.nonzero(bucket != n_buckets - 1)[0]
    return int(n[not_last[-1]]) + 2


def _shifted_bias(rel_bias, dist):
    nb, nh = rel_bias.shape
    rel = rel_bias.astype(F32) - rel_bias[nb - 1].astype(F32)
    bucket = _t5_bucket(jnp.maximum(dist, 0), nb)
    b = jnp.zeros((nh,) + dist.shape, F32)
    for k in range(nb - 1):
        b = jnp.where(bucket == k, rel[k].reshape((nh,) + (1,) * dist.ndim), b)
    return jnp.where(dist >= 0, b * LOG2E, NEG_INF)


def _gate_weights(w_gx, w_ga):
    nb, bw, _ = w_gx.shape
    per = LANES // bw

    def blockdiag(wb):
        wb = wb.reshape(nb // per, per, bw, bw)
        eye = jnp.eye(per, dtype=wb.dtype)
        return jnp.einsum('gpij,pq->gpiqj', wb, eye).reshape(nb // per, LANES, LANES)

    return jnp.concatenate([blockdiag(w_gx), blockdiag(w_ga)], axis=-1).astype(BF16)


def kernel(x_prompt, x_sample, p_prompt, p_sample, cache_k, cache_v, state_conv, state_rnn, page_table,
           rel_bias, g_mix, w_in, conv_w, conv_b, w_gx, b_gx, w_ga, b_ga, a_param,
           lam_q1, lam_k1, lam_q2, lam_k2, g_subln, b_merge, w_out,
           g_ffn, w_ffn_in, w_ffn_out, g_ple, w_ple_gate, w_ple_proj, g_final):
    bp, sp, d = x_prompt.shape
    bs, ts, _ = x_sample.shape
    depth, n_pool, page_size, n_heads, hw = cache_k.shape
    hd = hw // 2
    n_pages = page_table.shape[1]
    past = n_pages * page_size
    lru_w = conv_w.shape[2]
    kw = conv_w.shape[1]
    tl = TILES
    assert d == n_heads * hw == lru_w and w_in.shape[2] == 7 * d and hw == LANES
    assert ts == SUBLANES and kw - 1 <= min(ts, SUBLANES)

    far = _far_distance(rel_bias.shape[0])
    assert LANES + 1 >= far and page_size + 1 >= far

    r = jnp.arange(LANES, dtype=jnp.int32)
    diag = r[:, None] - r[None, :]
    bias_diag = _shifted_bias(rel_bias, diag)
    bias_prev = _shifted_bias(rel_bias, diag + LANES)
    tq = jnp.arange(ts, dtype=jnp.int32)
    kk = jnp.arange(page_size, dtype=jnp.int32)
    same_head = jnp.eye(n_heads, dtype=bool)[:, None, None, None, :]

    def head_table(b):
        full = jnp.where(same_head, b[:, None, :, :, None], NEG_INF)
        full = jnp.broadcast_to(full, (n_heads, 2, ts, b.shape[2], n_heads))
        return full.reshape(n_heads * 2 * ts, b.shape[2] * n_heads)

    mask = head_table(jnp.zeros((n_heads, ts, page_size), F32))
    mask_last = jnp.stack([mask, head_table(_shifted_bias(rel_bias, page_size + tq[:, None] - kk[None, :]))])
    bias_new = head_table(_shifted_bias(rel_bias, tq[:, None] - tq[None, :]))
    bias_new = jnp.pad(bias_new, ((0, 0), (0, LANES - ts * n_heads)), constant_values=NEG_INF)

    ck = cache_k.reshape(depth * n_pool, page_size * n_heads, hw)
    cv = cache_v.reshape(depth * n_pool, page_size * n_heads, hw)

    hp = x_prompt.reshape(bp * sp, d)
    hs = x_sample.reshape(bs * ts, d)
    row = lambda a: a.reshape(1, -1).astype(F32)
    outs = {k: [] for k in ("kp", "vp", "cp", "rp", "ks", "vs", "cs", "rs")}
    for l in range(depth):
        lam_init = 0.8 - 0.6 * math.exp(-0.3 * l)
        w_in_b = w_in[l].astype(BF16)
        wg = _gate_weights(w_gx[l], w_ga[l])
        lams = [row(a[l]) for a in (lam_q1, lam_k1, lam_q2, lam_k2)]
        gs = row(g_subln[l])
        rglru_w = (conv_w[l], row(conv_b[l]), wg, row(b_gx[l]), row(b_ga[l]), row(a_param[l]))
        post_w = (b_merge[l], w_out[l].astype(BF16), row(g_ffn[l]), w_ffn_in[l].astype(BF16),
                  w_ffn_out[l].astype(BF16), row(g_ple[l]), w_ple_gate[l].astype(BF16),
                  w_ple_proj[l].astype(BF16), row(g_final))
        last = l == depth - 1

        proj, k_rows, v_rows = _inproj(hp, row(g_mix[l]), w_in_b, tm=tl.inproj_rows, tn=d, k_col=1, v_col=2,
                                       n_heads=n_heads)
        r_p, hl_p = _rglru(proj, 3, 4, jnp.zeros((bp, SUBLANES, lru_w), F32), jnp.zeros((bp, 1, lru_w), F32),
                           *rglru_w, n_seq=bp, seq_len=sp, rows=tl.rglru_rows, reset_first=True)
        o_p = _attn_prompt(proj, bias_diag, bias_prev, lams, gs, n_seq=bp, seq_len=sp, n_heads=n_heads,
                           hd=hd, t=tl.attn_t, hps=tl.attn_heads, nt=tl.attn_tiles, lam_init=lam_init)
        hp = _post(hp, o_p, r_p, proj, 5, 6, p_prompt[l].reshape(bp * sp, -1), *post_w,
                   tm=tl.post_rows, final_norm=last)
        proj3 = proj.reshape(bp, sp, 7 * d)
        outs["kp"].append(k_rows.reshape(bp, sp, n_heads, hw))
        outs["vp"].append(v_rows.reshape(bp, sp, n_heads, hw))
        outs["cp"].append(proj3[:, sp - (kw - 1):, 3 * d:4 * d])
        outs["rp"].append(hl_p[:, SUBLANES - 1])

        proj, k_rows, v_rows = _inproj(hs, row(g_mix[l]), w_in_b, tm=bs * ts, tn=d, k_col=1, v_col=2,
                                       n_heads=n_heads)
        tail0 = jnp.pad(state_conv[l].astype(F32), ((0, 0), (SUBLANES - (kw - 1), 0), (0, 0)))
        r_s, hl_s = _rglru(proj, 3, 4, tail0, state_rnn[l].astype(F32)[:, None, :],
                           *rglru_w, n_seq=bs, seq_len=ts, rows=ts, reset_first=(past == 0))
        proj3 = proj.reshape(bs, ts, 7 * d)
        k_s = k_rows.reshape(bs, ts, n_heads, hw)
        v_s = v_rows.reshape(bs, ts, n_heads, hw)
        o_s = _attn_sample(page_table, proj, k_rows.reshape(bs, ts * n_heads, hw),
                           v_rows.reshape(bs, ts * n_heads, hw),
                           ck, cv, l * n_pool, mask, mask_last, bias_new, lams, gs,
                           n_seq=bs, t=ts, n_heads=n_heads, hd=hd, pages=tl.pages, group=tl.page_group, ring=tl.page_ring,
                           lam_init=lam_init)
        hs = _post(hs, o_s, r_s, proj, 5, 6, p_sample[l].reshape(bs * ts, -1), *post_w,
                   tm=bs * ts, final_norm=last)
        outs["ks"].append(k_s)
        outs["vs"].append(v_s)
        outs["cs"].append(proj3[:, ts - (kw - 1):, 3 * d:4 * d])
        outs["rs"].append(hl_s[:, SUBLANES - 1])

    st = lambda k: jnp.stack(outs[k])
    return (hp.reshape(bp, sp, d), hs.reshape(bs, ts, d),
            st("kp"), st("vp"), st("cp"), st("rp"), st("ks"), st("vs"), st("cs"), st("rs"))
```
